```python
import math
import jax, jax.numpy as jnp
from jax import lax
import numpy as np

D_MODEL = 2048
BATCH = 1
SEQ = 8192
DEPTH = 1
DEC_BATCH = 32
DEC_SEQ = 4
PAST_LEN = 16384
PAGE_SIZE = 128

N_HEADS = D_MODEL // 256
HEAD_DIM = 64
V_DIM = 2 * HEAD_DIM
D_ATTN = N_HEADS * V_DIM
D_GMLP = D_MODEL // 2
CHUNK = 128
GMLP_CH = 128
N_GROUPS = D_GMLP // GMLP_CH
D_FF = 4 * D_MODEL
Q_BLOCK = 128
Q_W = N_HEADS * 2 * HEAD_DIM
K_W = N_HEADS * 2 * HEAD_DIM
V_W = D_ATTN
G_W = 2 * D_GMLP
GATE_W = 2 * D_MODEL
D_IN = Q_W + K_W + V_W + G_W + GATE_W
SPLITS = (Q_W, Q_W + K_W, Q_W + K_W + V_W, Q_W + K_W + V_W + G_W)
SCALE = HEAD_DIM ** -0.5
NEG_INF = -1e30

kernel_name = "hybrid_diffattn_gmlp_decode_step"


def rms_norm(x, g, eps=1e-6):
    xf = x.astype(jnp.float32)
    y = xf * lax.rsqrt(jnp.mean(xf * xf, axis=-1, keepdims=True) + eps)
    return (y * g.astype(jnp.float32)).astype(x.dtype)


def layer_norm(x, g, b, eps=1e-5):
    xf = x.astype(jnp.float32)
    mu = jnp.mean(xf, axis=-1, keepdims=True)
    var = jnp.mean(jnp.square(xf - mu), axis=-1, keepdims=True)
    y = (xf - mu) * lax.rsqrt(var + eps)
    return (y * g.astype(jnp.float32) + b.astype(jnp.float32)).astype(x.dtype)


def diff_lambda(lq1, lk1, lq2, lk2, lam_init):
    f = jnp.float32
    return (jnp.exp(jnp.sum(lq1.astype(f) * lk1.astype(f)))
            - jnp.exp(jnp.sum(lq2.astype(f) * lk2.astype(f))) + lam_init)


def mixer_inputs(x, norm_g, w_in, ln_g, ln_b):
    b, s = x.shape[:2]
    h = rms_norm(x, norm_g)
    z = jnp.einsum('bsd,de->bse', h, w_in)
    q, k, v, zg, gates = jnp.split(z, SPLITS, axis=-1)
    q = q.reshape(b, s, N_HEADS, 2, HEAD_DIM)
    k = k.reshape(b, s, N_HEADS, 2, HEAD_DIM)
    v = v.reshape(b, s, N_HEADS, V_DIM)
    zg = jax.nn.gelu(zg)
    u, vs = jnp.split(zg, 2, axis=-1)
    vs = layer_norm(vs, ln_g, ln_b)
    return h, q, k, v, u, vs, gates


def diff_weights(sc, lam):
    p = jax.nn.softmax(sc, axis=-1)
    return p[:, :, 0] - lam * p[:, :, 1]


def attn_prompt(q, k, v, lam):
    b, s = q.shape[:2]
    nb = s // Q_BLOCK
    qb = jnp.moveaxis(q.reshape(b, nb, Q_BLOCK, N_HEADS, 2, HEAD_DIM), 1, 0)
    starts = jnp.arange(nb, dtype=jnp.int32) * Q_BLOCK
    kpos = jnp.arange(s, dtype=jnp.int32)

    def one_block(args):
        qi, start = args
        sc = jnp.einsum('bqhcd,bkhcd->bhcqk', qi, k).astype(jnp.float32) * SCALE
        qpos = start + jnp.arange(Q_BLOCK, dtype=jnp.int32)
        mask = kpos[None, :] <= qpos[:, None]
        sc = jnp.where(mask, sc, NEG_INF)
        pd = diff_weights(sc, lam).astype(v.dtype)
        return jnp.einsum('bhqk,bkhe->bqhe', pd, v)

    out = lax.map(one_block, (qb, starts))
    return jnp.moveaxis(out, 0, 1).reshape(b, s, N_HEADS, V_DIM)


def attn_sample(q, k_new, v_new, k_past, v_past, lam):
    t = q.shape[1]
    p_len = k_past.shape[1]
    s_past = jnp.einsum('bqhcd,bkhcd->bhcqk', q, k_past).astype(jnp.float32) * SCALE
    s_new = jnp.einsum('bqhcd,bkhcd->bhcqk', q, k_new).astype(jnp.float32) * SCALE
    tri = jnp.tril(jnp.ones((t, t), dtype=bool))
    s_new = jnp.where(tri, s_new, NEG_INF)
    pd = diff_weights(jnp.concatenate([s_past, s_new], axis=-1), lam).astype(v_new.dtype)
    return (jnp.einsum('bhqk,bkhe->bqhe', pd[..., :p_len], v_past)
            + jnp.einsum('bhqk,bkhe->bqhe', pd[..., p_len:], v_new))


def chunk_spatial(v, w_s, b_s):
    t = v.shape[2]
    w = jnp.tril(w_s[:, :t, :t])
    bias = jnp.swapaxes(b_s[:, :t], 0, 1)[:, :, None]
    return jnp.einsum('gts,bnsgc->bntgc', w, v) + bias


def mixer_out(att, u, sg, gates, subln_g, lam_init, w_ba, w_bb, w_out):
    b, s = att.shape[:2]
    att = rms_norm(att, subln_g) * (1.0 - lam_init)
    a_proj = jnp.einsum('bse,ed->bsd', att.reshape(b, s, D_ATTN), w_ba)
    g_proj = jnp.einsum('bse,ed->bsd', u * sg, w_bb)
    ga, gb = jnp.split(gates, 2, axis=-1)
    m = jax.nn.sigmoid(ga) * a_proj + jax.nn.sigmoid(gb) * g_proj
    return jnp.einsum('bsd,de->bse', m, w_out)


def ffn(x, norm_g, w1, w2):
    h = rms_norm(x, norm_g)
    a = jnp.einsum('bsd,df->bsf', h, w1)
    return jnp.einsum('bsf,fd->bsd', jnp.square(jax.nn.relu(a)), w2)


def setup_inputs(seed: int = 0) -> dict:
    key = jax.random.key(seed)
    ks = jax.random.split(key, 24)
    f32 = jnp.float32
    n_pages = PAST_LEN // PAGE_SIZE
    n_used = DEC_BATCH * n_pages
    n_pool = (n_used * 5) // 4
    nrm = lambda k, shp, sc: jax.random.normal(k, shp, f32) * sc
    page_table = jax.random.permutation(ks[0], n_pool)[:n_used].reshape(DEC_BATCH, n_pages).astype(jnp.int32)
    return {
        "x_prompt": nrm(ks[1], (BATCH, SEQ, D_MODEL), 1.0),
        "x_sample": nrm(ks[2], (DEC_BATCH, DEC_SEQ, D_MODEL), 1.0),
        "cache_k": nrm(ks[3], (DEPTH, n_pool, PAGE_SIZE, N_HEADS, 2 * HEAD_DIM), 1.0),
        "cache_v": nrm(ks[4], (DEPTH, n_pool, PAGE_SIZE, N_HEADS, V_DIM), 1.0),
        "page_table": page_table,
        "norm_mix_g": 1.0 + nrm(ks[5], (DEPTH, D_MODEL), 0.02),
        "w_in": nrm(ks[6], (DEPTH, D_MODEL, D_IN), D_MODEL ** -0.5),
        "lambda_q1": nrm(ks[7], (DEPTH, HEAD_DIM), 0.1),
        "lambda_k1": nrm(ks[8], (DEPTH, HEAD_DIM), 0.1),
        "lambda_q2": nrm(ks[9], (DEPTH, HEAD_DIM), 0.1),
        "lambda_k2": nrm(ks[10], (DEPTH, HEAD_DIM), 0.1),
        "subln_g": 1.0 + nrm(ks[11], (DEPTH, V_DIM), 0.02),
        "gmlp_ln_g": 1.0 + nrm(ks[12], (DEPTH, D_GMLP), 0.02),
        "gmlp_ln_b": nrm(ks[13], (DEPTH, D_GMLP), 0.02),
        "w_spatial": nrm(ks[14], (DEPTH, N_GROUPS, CHUNK, CHUNK), CHUNK ** -0.5),
        "b_spatial": 1.0 + nrm(ks[15], (DEPTH, N_GROUPS, CHUNK), 0.01),
        "w_branch_a": nrm(ks[16], (DEPTH, D_ATTN, D_MODEL), D_ATTN ** -0.5),
        "w_branch_b": nrm(ks[17], (DEPTH, D_GMLP, D_MODEL), D_GMLP ** -0.5),
        "w_out": nrm(ks[18], (DEPTH, D_MODEL, D_MODEL), D_MODEL ** -0.5),
        "norm_ffn_g": 1.0 + nrm(ks[19], (DEPTH, D_MODEL), 0.02),
        "w_ff1": nrm(ks[20], (DEPTH, D_MODEL, D_FF), D_MODEL ** -0.5),
        "w_ff2": nrm(ks[21], (DEPTH, D_FF, D_MODEL), D_FF ** -0.5),
        "norm_final_g": 1.0 + nrm(ks[22], (D_MODEL,), 0.02),
    }


def reference(x_prompt, x_sample, cache_k, cache_v, page_table, norm_mix_g, w_in,
              lambda_q1, lambda_k1, lambda_q2, lambda_k2, subln_g, gmlp_ln_g, gmlp_ln_b,
              w_spatial, b_spatial, w_branch_a, w_branch_b, w_out, norm_ffn_g,
              w_ff1, w_ff2, norm_final_g):
    yp, ys = x_prompt, x_sample
    bp, sp = x_prompt.shape[:2]
    bs, ts = x_sample.shape[:2]
    n_pages = page_table.shape[1]
    kp_rows, vp_rows, ks_rows, vs_rows, gs_rows = [], [], [], [], []
    for l in range(DEPTH):
        lam_init = 0.8 - 0.6 * math.exp(-0.3 * l)
        lam = diff_lambda(lambda_q1[l], lambda_k1[l], lambda_q2[l], lambda_k2[l], lam_init)

        _, q, k, v, u, vg, gates = mixer_inputs(yp, norm_mix_g[l], w_in[l], gmlp_ln_g[l], gmlp_ln_b[l])
        att = attn_prompt(q, k, v, lam)
        sg = chunk_spatial(vg.reshape(bp, sp // CHUNK, CHUNK, N_GROUPS, GMLP_CH),
                           w_spatial[l], b_spatial[l]).reshape(bp, sp, D_GMLP)
        yp = yp + mixer_out(att, u, sg, gates, subln_g[l], lam_init, w_branch_a[l], w_branch_b[l], w_out[l])
        yp = yp + ffn(yp, norm_ffn_g[l], w_ff1[l], w_ff2[l])
        kp_rows.append(k.reshape(bp, sp, N_HEADS, 2 * HEAD_DIM))
        vp_rows.append(v)

        _, q, k, v, u, vg, gates = mixer_inputs(ys, norm_mix_g[l], w_in[l], gmlp_ln_g[l], gmlp_ln_b[l])
        k_past = cache_k[l][page_table].reshape(bs, n_pages * PAGE_SIZE, N_HEADS, 2, HEAD_DIM)
        v_past = cache_v[l][page_table].reshape(bs, n_pages * PAGE_SIZE, N_HEADS, V_DIM)
        att = attn_sample(q, k, v, k_past, v_past, lam)
        sg = chunk_spatial(vg.reshape(bs, 1, ts, N_GROUPS, GMLP_CH),
                           w_spatial[l], b_spatial[l]).reshape(bs, ts, D_GMLP)
        ys = ys + mixer_out(att, u, sg, gates, subln_g[l], lam_init, w_branch_a[l], w_branch_b[l], w_out[l])
        ys = ys + ffn(ys, norm_ffn_g[l], w_ff1[l], w_ff2[l])
        ks_rows.append(k.reshape(bs, ts, N_HEADS, 2 * HEAD_DIM))
        vs_rows.append(v)
        gs_rows.append(vg)

    yp = rms_norm(yp, norm_final_g)
    ys = rms_norm(ys, norm_final_g)
    return (yp, ys, jnp.stack(kp_rows), jnp.stack(vp_rows), jnp.stack(ks_rows),
            jnp.stack(vs_rows), jnp.stack(gs_rows))
```

```python
import functools
import math

import jax
import jax.numpy as jnp
from jax import lax
from jax.experimental import pallas as pl
from jax.experimental.pallas import tpu as pltpu

F32 = jnp.float32
BF16 = jnp.bfloat16

HEAD_DIM = 64
V_DIM = 2 * HEAD_DIM
CHUNK = 128
NEG_INF = -1e30
RMS_EPS = 1e-6
LN_EPS = 1e-5
MIB = 1024 * 1024
V7X_VMEM_BUDGET = 56 * MIB


def _row_tile(m, target):
    t = min(m, target)
    assert m % t == 0
    return t


def _params(sem, vmem=V7X_VMEM_BUDGET):
    return pltpu.CompilerParams(dimension_semantics=sem, vmem_limit_bytes=vmem)


def _rms(x, g):
    return x * lax.rsqrt(jnp.mean(x * x, axis=-1, keepdims=True) + RMS_EPS) * g


def _lambda(lq1, lk1, lq2, lk2, lam_init):
    a = jnp.exp(jnp.sum(lq1[...] * lk1[...], axis=1, keepdims=True))
    b = jnp.exp(jnp.sum(lq2[...] * lk2[...], axis=1, keepdims=True))
    return a - b + lam_init


def _rmsnorm_kernel(x_ref, g_ref, o_ref):
    o_ref[...] = _rms(x_ref[...], g_ref[...]).astype(o_ref.dtype)


def _rmsnorm(x, g, tm):
    m, d = x.shape
    return pl.pallas_call(
        _rmsnorm_kernel,
        out_shape=jax.ShapeDtypeStruct((m, d), BF16),
        grid=(m // tm,),
        in_specs=[pl.BlockSpec((tm, d), lambda i: (i, 0)),
                  pl.BlockSpec((1, d), lambda i: (0, 0))],
        out_specs=pl.BlockSpec((tm, d), lambda i: (i, 0)),
        compiler_params=_params(("arbitrary",)),
        name="rmsnorm",
    )(x, g)


def _inproj_kernel(h_ref, w_ref, *refs, kind, scale):
    wbf_ref = refs[-1]

    @pl.when(pl.program_id(1) == 0)
    def _cast_weights():
        wbf_ref[...] = w_ref[...].astype(BF16)

    z = jnp.dot(h_ref[...], wbf_ref[...], preferred_element_type=F32)
    if kind == "q":
        refs[0][...] = (z * scale).astype(BF16)
    elif kind == "kv":
        refs[0][...] = z
        refs[1][...] = z.astype(BF16)
    elif kind == "u":
        refs[0][...] = jax.nn.gelu(z)
    elif kind == "vs":
        g_ref, b_ref, o_ref = refs[0], refs[1], refs[2]
        y = jax.nn.gelu(z)
        mu = jnp.mean(y, axis=-1, keepdims=True)
        yc = y - mu
        var = jnp.mean(yc * yc, axis=-1, keepdims=True)
        o_ref[...] = (yc * lax.rsqrt(var + LN_EPS) * g_ref[...] + b_ref[...]).astype(o_ref.dtype)
    elif kind == "gates":
        refs[0][...] = jax.nn.sigmoid(z)
    else:
        raise ValueError(kind)


def _inproj(h, w, col0, ncols, kind, out_dtypes, tm, extra=(), scale=1.0, tn=1024):
    m, d = h.shape
    assert col0 % tn == 0 and ncols % tn == 0
    j0 = col0 // tn
    in_specs = [pl.BlockSpec((tm, d), lambda j, i: (i, 0)),
                pl.BlockSpec((d, tn), lambda j, i: (0, j0 + j))]
    in_specs += [pl.BlockSpec((1, tn), lambda j, i: (0, 0)) for _ in extra]
    outs = pl.pallas_call(
        functools.partial(_inproj_kernel, kind=kind, scale=scale),
        out_shape=[jax.ShapeDtypeStruct((m, ncols), dt) for dt in out_dtypes],
        grid=(ncols // tn, m // tm),
        in_specs=in_specs,
        out_specs=[pl.BlockSpec((tm, tn), lambda j, i: (i, j)) for _ in out_dtypes],
        scratch_shapes=[pltpu.VMEM((d, tn), BF16)],
        compiler_params=_params(("arbitrary", "arbitrary")),
        name="inproj_" + kind,
    )(h, w, *extra)
    return outs


def _attn_prompt_kernel(lq1, lk1, lq2, lk2, sg_ref, q_ref, k_ref, v_ref, o_ref,
                        m_ref, l_ref, acc_ref, *, tq, lam_init):
    qi = pl.program_id(1)
    q = q_ref[...]
    lane = lax.broadcasted_iota(jnp.int32, q.shape, 1)
    zero = jnp.zeros_like(q)
    qb = jnp.concatenate([jnp.where(lane < HEAD_DIM, q, zero),
                          jnp.where(lane >= HEAD_DIM, q, zero)], axis=0)

    m_ref[...] = jnp.full(m_ref.shape, NEG_INF, F32)
    l_ref[...] = jnp.zeros(l_ref.shape, F32)
    acc_ref[...] = jnp.zeros(acc_ref.shape, F32)

    def step(ki, masked):
        k = k_ref[pl.ds(pl.multiple_of(ki * tq, tq), tq), :]
        v = v_ref[pl.ds(pl.multiple_of(ki * tq, tq), tq), :]
        s = lax.dot_general(qb, k, (((1,), (1,)), ((), ())), preferred_element_type=F32)
        if masked:
            row = lax.broadcasted_iota(jnp.int32, s.shape, 0)
            row = jnp.where(row >= tq, row - tq, row)
            col = lax.broadcasted_iota(jnp.int32, s.shape, 1)
            s = jnp.where(col <= row, s, NEG_INF)
        m_old = m_ref[...]
        m_new = jnp.maximum(m_old, jnp.max(s, axis=1, keepdims=True))
        alpha = jnp.exp(m_old - m_new)
        p = jnp.exp(s - m_new)
        l_ref[...] = alpha * l_ref[...] + jnp.sum(p, axis=1, keepdims=True)
        acc_ref[...] = alpha * acc_ref[...] + jnp.dot(p.astype(BF16), v, preferred_element_type=F32)
        m_ref[...] = m_new

    def body(ki, carry):
        step(ki, False)
        return carry

    lax.fori_loop(0, qi, body, 0)
    step(qi, True)

    n = acc_ref[...] / l_ref[...]
    lam = _lambda(lq1, lk1, lq2, lk2, lam_init)
    att = n[:tq] - lam * n[tq:]
    o_ref[...] = (_rms(att, sg_ref[...]) * (1.0 - lam_init)).astype(o_ref.dtype)


def _attn_prompt(q, k, v, lams, subln_g, lam_init, tq):
    s, w = q.shape
    n_heads = w // V_DIM
    lam_specs = [pl.BlockSpec((1, HEAD_DIM), lambda h, i: (0, 0)) for _ in lams]
    return pl.pallas_call(
        functools.partial(_attn_prompt_kernel, tq=tq, lam_init=lam_init),
        out_shape=jax.ShapeDtypeStruct((s, w), BF16),
        grid=(n_heads, s // tq),
        in_specs=lam_specs + [
            pl.BlockSpec((1, V_DIM), lambda h, i: (0, 0)),
            pl.BlockSpec((tq, V_DIM), lambda h, i: (i, h)),
            pl.BlockSpec((s, V_DIM), lambda h, i: (0, h)),
            pl.BlockSpec((s, V_DIM), lambda h, i: (0, h)),
        ],
        out_specs=pl.BlockSpec((tq, V_DIM), lambda h, i: (i, h)),
        scratch_shapes=[pltpu.VMEM((2 * tq, 1), F32), pltpu.VMEM((2 * tq, 1), F32),
                        pltpu.VMEM((2 * tq, V_DIM), F32)],
        compiler_params=_params(("arbitrary", "arbitrary")),
        name="attn_prompt",
    )(*lams, subln_g, q, k, v)


def _attn_decode_kernel(pt_ref, lq1, lk1, lq2, lk2, sg_ref, q_ref, kn_ref, vn_ref, *refs,
                        pages, t_new, n_heads, lam_init):
    k_refs = refs[:pages]
    v_refs = refs[pages:2 * pages]
    o_ref, m_ref, l_ref, acc_ref = refs[2 * pages:]
    t = pl.program_id(1)
    q = q_ref[0]
    rows_per_head = 2 * t_new
    nt = (((1,), (1,)), ((), ()))

    def own_head(shape):
        row = lax.broadcasted_iota(jnp.int32, shape, 0)
        col = lax.broadcasted_iota(jnp.int32, shape, 1)
        return (col % n_heads) == (row // rows_per_head), row, col

    @pl.when(t == 0)
    def _init_from_new_tokens():
        s = lax.dot_general(q, kn_ref[0], nt, preferred_element_type=F32)
        same, row, col = own_head(s.shape)
        s = jnp.where(same & ((col // n_heads) <= (row % t_new)), s, NEG_INF)
        m = jnp.max(s, axis=1, keepdims=True)
        p = jnp.exp(s - m)
        m_ref[...] = m
        l_ref[...] = jnp.sum(p, axis=1, keepdims=True)
        acc_ref[...] = jnp.dot(p.astype(BF16), vn_ref[0], preferred_element_type=F32)

    s = jnp.concatenate(
        [lax.dot_general(q, k_refs[i][0].astype(BF16), nt, preferred_element_type=F32)
         for i in range(pages)], axis=1)
    same, _, _ = own_head(s.shape)
    s = jnp.where(same, s, NEG_INF)
    m_old = m_ref[...]
    m_new = jnp.maximum(m_old, jnp.max(s, axis=1, keepdims=True))
    alpha = jnp.exp(m_old - m_new)
    p = jnp.exp(s - m_new)
    l_ref[...] = alpha * l_ref[...] + jnp.sum(p, axis=1, keepdims=True)
    pb = p.astype(BF16)
    w = k_refs[0].shape[1]
    pv = jnp.dot(pb[:, :w], v_refs[0][0].astype(BF16), preferred_element_type=F32)
    for i in range(1, pages):
        pv += jnp.dot(pb[:, i * w:(i + 1) * w], v_refs[i][0].astype(BF16),
                      preferred_element_type=F32)
    acc_ref[...] = alpha * acc_ref[...] + pv
    m_ref[...] = m_new

    @pl.when(t == pl.num_programs(1) - 1)
    def _finalize():
        lam = _lambda(lq1, lk1, lq2, lk2, lam_init)
        n = acc_ref[...] / l_ref[...]
        for h in range(n_heads):
            r0 = h * rows_per_head
            att = n[r0:r0 + t_new] - lam * n[r0 + t_new:r0 + rows_per_head]
            o_ref[0, :, h * V_DIM:(h + 1) * V_DIM] = _rms(att, sg_ref[...]) * (1.0 - lam_init)


def _attn_decode(q, k_new, v_new, cache_k, cache_v, page_table, lams, subln_g, lam_init, t_new,
                 n_heads, pages):
    bs, r, _ = q.shape
    n_pages = page_table.shape[1]
    assert n_pages % pages == 0
    pt_flat = page_table.reshape(-1)
    page_rows = cache_k.shape[1]

    def cache_spec(i):
        return pl.BlockSpec((1, page_rows, V_DIM),
                            lambda b, t, pt: (pt[b * n_pages + t * pages + i], 0, 0))

    const = lambda b, t, pt: (0, 0)
    per_seq = lambda b, t, pt: (b, 0, 0)
    grid_spec = pltpu.PrefetchScalarGridSpec(
        num_scalar_prefetch=1,
        grid=(bs, n_pages // pages),
        in_specs=[pl.BlockSpec((1, HEAD_DIM), const) for _ in lams] + [
            pl.BlockSpec((1, V_DIM), const),
            pl.BlockSpec((1, r, V_DIM), per_seq),
            pl.BlockSpec((1,) + k_new.shape[1:], per_seq),
            pl.BlockSpec((1,) + v_new.shape[1:], per_seq),
        ] + [cache_spec(i) for i in range(pages)] + [cache_spec(i) for i in range(pages)],
        out_specs=pl.BlockSpec((1, t_new, n_heads * V_DIM), per_seq),
        scratch_shapes=[pltpu.VMEM((r, 1), F32), pltpu.VMEM((r, 1), F32), pltpu.VMEM((r, V_DIM), F32)],
    )
    return pl.pallas_call(
        functools.partial(_attn_decode_kernel, pages=pages, t_new=t_new, n_heads=n_heads,
                          lam_init=lam_init),
        out_shape=jax.ShapeDtypeStruct((bs, t_new, n_heads * V_DIM), F32),
        grid_spec=grid_spec,
        compiler_params=_params(("arbitrary", "arbitrary")),
        name="attn_decode",
    )(pt_flat, *lams, subln_g, q, k_new, v_new, *([cache_k] * pages), *([cache_v] * pages))


def _gating_kernel(u_ref, vg_ref, w_ref, b_ref, o_ref, *, chunk_len):
    tm = u_ref.shape[0]
    n_groups = w_ref.shape[0]
    row = lax.broadcasted_iota(jnp.int32, (CHUNK, CHUNK), 0)
    col = lax.broadcasted_iota(jnp.int32, (CHUNK, CHUNK), 1)
    mask = (row // chunk_len == col // chunk_len) & (col <= row)
    for g in range(n_groups):
        wg = jnp.where(mask, w_ref[g], 0.0).astype(BF16)
        bg = b_ref[g]
        cs = slice(g * CHUNK, (g + 1) * CHUNK)
        for r in range(tm // CHUNK):
            rs = slice(r * CHUNK, (r + 1) * CHUNK)
            sg = jnp.dot(wg, vg_ref[rs, cs].astype(BF16), preferred_element_type=F32) + bg
            o_ref[rs, cs] = (u_ref[rs, cs] * sg).astype(o_ref.dtype)


def _gating(u, vg, wsp, bsp, chunk_len, tm):
    m, d = u.shape
    g = wsp.shape[0]
    return pl.pallas_call(
        functools.partial(_gating_kernel, chunk_len=chunk_len),
        out_shape=jax.ShapeDtypeStruct((m, d), BF16),
        grid=(m // tm,),
        in_specs=[pl.BlockSpec((tm, d), lambda i: (i, 0)),
                  pl.BlockSpec((tm, d), lambda i: (i, 0)),
                  pl.BlockSpec((g, CHUNK, CHUNK), lambda i: (0, 0, 0)),
                  pl.BlockSpec((g, CHUNK, 1), lambda i: (0, 0, 0))],
        out_specs=pl.BlockSpec((tm, d), lambda i: (i, 0)),
        compiler_params=_params(("arbitrary",)),
        name="spatial_gating",
    )(u, vg, wsp, bsp)


def _branch_kernel(att_ref, ug_ref, ga_ref, gb_ref, wa_ref, wb_ref, m_ref):
    a = jnp.dot(att_ref[...].astype(BF16), wa_ref[...], preferred_element_type=F32)
    g = jnp.dot(ug_ref[...], wb_ref[...], preferred_element_type=F32)
    m_ref[...] = (ga_ref[...] * a + gb_ref[...] * g).astype(m_ref.dtype)


def _branch(att, ug, gates, wa, wb, tm, tn=1024):
    m, d_in = att.shape
    d = wa.shape[1]
    nj = d // tn
    return pl.pallas_call(
        _branch_kernel,
        out_shape=jax.ShapeDtypeStruct((m, d), BF16),
        grid=(nj, m // tm),
        in_specs=[pl.BlockSpec((tm, d_in), lambda j, i: (i, 0)),
                  pl.BlockSpec((tm, d_in), lambda j, i: (i, 0)),
                  pl.BlockSpec((tm, tn), lambda j, i: (i, j)),
                  pl.BlockSpec((tm, tn), lambda j, i: (i, nj + j)),
                  pl.BlockSpec((d_in, tn), lambda j, i: (0, j)),
                  pl.BlockSpec((d_in, tn), lambda j, i: (0, j))],
        out_specs=pl.BlockSpec((tm, tn), lambda j, i: (i, j)),
        compiler_params=_params(("arbitrary", "arbitrary")),
        name="branch_merge",
    )(att, ug, gates, gates, wa, wb)


def _outproj_kernel(m_ref, x_ref, w_ref, g_ref, y_ref, h_ref):
    y = x_ref[...] + jnp.dot(m_ref[...], w_ref[...], preferred_element_type=F32)
    y_ref[...] = y
    h_ref[...] = _rms(y, g_ref[...]).astype(h_ref.dtype)


def _outproj(mm, x, w, g, tm):
    m, d = x.shape
    return pl.pallas_call(
        _outproj_kernel,
        out_shape=[jax.ShapeDtypeStruct((m, d), F32), jax.ShapeDtypeStruct((m, d), BF16)],
        grid=(m // tm,),
        in_specs=[pl.BlockSpec((tm, d), lambda i: (i, 0)),
                  pl.BlockSpec((tm, d), lambda i: (i, 0)),
                  pl.BlockSpec((d, d), lambda i: (0, 0)),
                  pl.BlockSpec((1, d), lambda i: (0, 0))],
        out_specs=[pl.BlockSpec((tm, d), lambda i: (i, 0)),
                   pl.BlockSpec((tm, d), lambda i: (i, 0))],
        compiler_params=_params(("arbitrary",)),
        name="outproj",
    )(mm, x, w, g)


def _ffn_kernel(h_ref, y_ref, w1_ref, w2_ref, g_ref, o_ref):
    f = pl.program_id(1)

    @pl.when(f == 0)
    def _init():
        o_ref[...] = y_ref[...]

    a = jnp.dot(h_ref[...], w1_ref[...], preferred_element_type=F32)
    r = jnp.square(jnp.maximum(a, 0.0)).astype(BF16)
    o_ref[...] += jnp.dot(r, w2_ref[...], preferred_element_type=F32)

    @pl.when(f == pl.num_programs(1) - 1)
    def _final_norm():
        o_ref[...] = _rms(o_ref[...], g_ref[...])


def _ffn(h, y, w1, w2, g, tm, tf=512):
    m, d = h.shape
    d_ff = w1.shape[1]
    return pl.pallas_call(
        _ffn_kernel,
        out_shape=jax.ShapeDtypeStruct((m, d), F32),
        grid=(m // tm, d_ff // tf),
        in_specs=[pl.BlockSpec((tm, d), lambda i, f: (i, 0)),
                  pl.BlockSpec((tm, d), lambda i, f: (i, 0)),
                  pl.BlockSpec((d, tf), lambda i, f: (0, f)),
                  pl.BlockSpec((tf, d), lambda i, f: (f, 0)),
                  pl.BlockSpec((1, d), lambda i, f: (0, 0))],
        out_specs=pl.BlockSpec((tm, d), lambda i, f: (i, 0)),
        compiler_params=_params(("arbitrary", "arbitrary")),
        name="ffn",
    )(h, y, w1, w2, g)


def _token_path_in(x, norm_g, w_in, ln_g, ln_b, d_attn, d_gmlp, tm, vg_dtype):
    d_model = x.shape[1]
    h = _rmsnorm(x, norm_g, tm)
    c = 0
    (q,) = _inproj(h, w_in, c, d_attn, "q", [BF16], tm, scale=HEAD_DIM ** -0.5)
    c += d_attn
    k32, k16 = _inproj(h, w_in, c, d_attn, "kv", [F32, BF16], tm)
    c += d_attn
    v32, v16 = _inproj(h, w_in, c, d_attn, "kv", [F32, BF16], tm)
    c += d_attn
    (u,) = _inproj(h, w_in, c, d_gmlp, "u", [F32], tm)
    c += d_gmlp
    (vg,) = _inproj(h, w_in, c, d_gmlp, "vs", [vg_dtype], tm, extra=(ln_g, ln_b))
    c += d_gmlp
    (gates,) = _inproj(h, w_in, c, 2 * d_model, "gates", [F32], tm)
    return q, k32, k16, v32, v16, u, vg, gates


def _token_path_out(x, att, u, vg, gates, wsp, bsp, chunk_len, wa, wb, wo, norm_ffn_g, w1, w2,
                    norm_final_g, tm):
    ug = _gating(u, vg, wsp, bsp, chunk_len, tm)
    mm = _branch(att, ug, gates, wa, wb, tm)
    y, h2 = _outproj(mm, x, wo, norm_ffn_g, tm)
    return _ffn(h2, y, w1, w2, norm_final_g, tm)


def kernel(x_prompt, x_sample, cache_k, cache_v, page_table, norm_mix_g, w_in, lambda_q1, lambda_k1,
           lambda_q2, lambda_k2, subln_g, gmlp_ln_g, gmlp_ln_b, w_spatial, b_spatial, w_branch_a,
           w_branch_b, w_out, norm_ffn_g, w_ff1, w_ff2, norm_final_g):
    bp, sp, d_model = x_prompt.shape
    bs, ts, _ = x_sample.shape
    depth = w_in.shape[0]
    assert bp == 1 and depth == 1, "the final norm is fused into the MLP kernel of the single layer"
    n_heads = cache_k.shape[3]
    d_attn = n_heads * V_DIM
    d_gmlp = gmlp_ln_g.shape[1]
    n_groups = w_spatial.shape[1]
    page = cache_k.shape[2]
    assert sp % CHUNK == 0 and CHUNK % ts == 0 and (bs * ts) % CHUNK == 0

    l = 0
    lam_init = 0.8 - 0.6 * math.exp(-0.3 * l)
    row2 = lambda a: a.reshape(1, -1)
    lams = [row2(lambda_q1[l]), row2(lambda_k1[l]), row2(lambda_q2[l]), row2(lambda_k2[l])]
    sub_g = row2(subln_g[l])
    ln_g, ln_b = row2(gmlp_ln_g[l]), row2(gmlp_ln_b[l])
    mix_g, ffn_g, fin_g = row2(norm_mix_g[l]), row2(norm_ffn_g[l]), row2(norm_final_g)
    wa, wb, wo = (w_branch_a[l].astype(BF16), w_branch_b[l].astype(BF16), w_out[l].astype(BF16))
    w1, w2 = w_ff1[l].astype(BF16), w_ff2[l].astype(BF16)

    xp = x_prompt.reshape(sp, d_model)
    tm_p = _row_tile(sp, 512)
    q, k32, k16, v32, v16, u, vg, gates = _token_path_in(
        xp, mix_g, w_in[l], ln_g, ln_b, d_attn, d_gmlp, tm_p, BF16)
    att = _attn_prompt(q, k16, v16, lams, sub_g, lam_init, _row_tile(sp, 512))
    bsp_p = b_spatial[l].reshape(n_groups, CHUNK, 1)
    yp = _token_path_out(xp, att, u, vg, gates, w_spatial[l], bsp_p, CHUNK, wa, wb, wo, ffn_g,
                         w1, w2, fin_g, tm_p)
    new_k_prompt = k32.reshape(1, bp, sp, n_heads, V_DIM)
    new_v_prompt = v32.reshape(1, bp, sp, n_heads, V_DIM)

    ms = bs * ts
    xs = x_sample.reshape(ms, d_model)
    tm_s = _row_tile(ms, 512)
    q, k32, k16, v32, v16, u, vg, gates = _token_path_in(
        xs, mix_g, w_in[l], ln_g, ln_b, d_attn, d_gmlp, tm_s, F32)
    q5 = q.reshape(bs, ts, n_heads, 2, HEAD_DIM).transpose(0, 2, 3, 1, 4)
    eye = jnp.eye(2, dtype=BF16)
    q_rows = (q5[:, :, :, :, None, :] * eye[None, None, :, None, :, None]).reshape(
        bs, n_heads * 2 * ts, V_DIM)
    k_new = k16.reshape(bs, ts * n_heads, V_DIM)
    v_new = v16.reshape(bs, ts * n_heads, V_DIM)
    ck = cache_k[l].reshape(-1, page * n_heads, V_DIM)
    cv = cache_v[l].reshape(-1, page * n_heads, V_DIM)
    n_pages = page_table.shape[1]
    pages = 8 if n_pages % 8 == 0 else 1
    att_s = _attn_decode(q_rows, k_new, v_new, ck, cv, page_table, lams, sub_g, lam_init, ts,
                         n_heads, pages)
    att_s = att_s.reshape(ms, d_attn)
    reps = CHUNK // ts
    wsp_s = jnp.tile(w_spatial[l][:, :ts, :ts], (1, reps, reps))
    bsp_s = jnp.tile(b_spatial[l][:, :ts], (1, reps)).reshape(n_groups, CHUNK, 1)
    ys = _token_path_out(xs, att_s, u, vg, gates, wsp_s, bsp_s, ts, wa, wb, wo, ffn_g,
                         w1, w2, fin_g, tm_s)

    return (yp.reshape(bp, sp, d_model), ys.reshape(bs, ts, d_model), new_k_prompt, new_v_prompt,
            k32.reshape(1, bs, ts, n_heads, V_DIM), v32.reshape(1, bs, ts, n_heads, V_DIM),
            vg.reshape(1, bs, ts, d_gmlp))
```

```python
import functools
import math

import jax
import jax.numpy as jnp
from jax import lax
from jax.experimental import pallas as pl
from jax.experimental.pallas import tpu as pltpu

F32 = jnp.float32
BF16 = jnp.bfloat16

HEAD_DIM = 64
V_DIM = 2 * HEAD_DIM
CHUNK = 128
NEG_INF = -1e30
LOG2_E = math.log2(math.e)
RMS_EPS = 1e-6
LN_EPS = 1e-5
MIB = 1024 * 1024
V7X_VMEM_BUDGET = 56 * MIB


def _row_tile(m, target):
    t = min(m, target)
    assert m % t == 0
    return t


def _params(sem, vmem=V7X_VMEM_BUDGET):
    return pltpu.CompilerParams(dimension_semantics=sem, vmem_limit_bytes=vmem)


def _rms(x, g):
    return x * lax.rsqrt(jnp.mean(x * x, axis=-1, keepdims=True) + RMS_EPS) * g


def _lambda(lq1, lk1, lq2, lk2, lam_init):
    a = jnp.exp(jnp.sum(lq1[...] * lk1[...], axis=1, keepdims=True))
    b = jnp.exp(jnp.sum(lq2[...] * lk2[...], axis=1, keepdims=True))
    return a - b + lam_init


def _rmsnorm_kernel(x_ref, g_ref, o_ref):
    o_ref[...] = _rms(x_ref[...], g_ref[...]).astype(o_ref.dtype)


def _rmsnorm(x, g, tm):
    m, d = x.shape
    return pl.pallas_call(
        _rmsnorm_kernel,
        out_shape=jax.ShapeDtypeStruct((m, d), BF16),
        grid=(m // tm,),
        in_specs=[pl.BlockSpec((tm, d), lambda i: (i, 0)),
                  pl.BlockSpec((1, d), lambda i: (0, 0))],
        out_specs=pl.BlockSpec((tm, d), lambda i: (i, 0)),
        compiler_params=_params(("arbitrary",)),
        name="rmsnorm",
    )(x, g)


def _inproj_kernel(h_ref, w_ref, *refs, kind, scale):
    wbf_ref = refs[-1]

    @pl.when(pl.program_id(1) == 0)
    def _cast_weights():
        wbf_ref[...] = w_ref[...].astype(BF16)

    z = jnp.dot(h_ref[...], wbf_ref[...], preferred_element_type=F32)
    if kind == "q":
        refs[0][...] = (z * scale).astype(BF16)
    elif kind == "kv":
        refs[0][...] = z
        refs[1][...] = z.astype(BF16)
    elif kind == "vt":
        refs[0][...] = z
        refs[1][0] = z.T.astype(BF16)
    elif kind == "u":
        refs[0][...] = jax.nn.gelu(z)
    elif kind == "vs":
        g_ref, b_ref, o_ref = refs[0], refs[1], refs[2]
        y = jax.nn.gelu(z)
        mu = jnp.mean(y, axis=-1, keepdims=True)
        yc = y - mu
        var = jnp.mean(yc * yc, axis=-1, keepdims=True)
        o_ref[...] = (yc * lax.rsqrt(var + LN_EPS) * g_ref[...] + b_ref[...]).astype(o_ref.dtype)
    elif kind == "gates":
        refs[0][...] = jax.nn.sigmoid(z)
    else:
        raise ValueError(kind)


def _inproj(h, w, col0, ncols, kind, out_dtypes, tm, extra=(), scale=1.0, tn=1024):
    m, d = h.shape
    assert col0 % tn == 0 and ncols % tn == 0
    j0 = col0 // tn
    in_specs = [pl.BlockSpec((tm, d), lambda j, i: (i, 0)),
                pl.BlockSpec((d, tn), lambda j, i: (0, j0 + j))]
    in_specs += [pl.BlockSpec((1, tn), lambda j, i: (0, 0)) for _ in extra]
    out_shape = [jax.ShapeDtypeStruct((m, ncols), dt) for dt in out_dtypes]
    out_specs = [pl.BlockSpec((tm, tn), lambda j, i: (i, j)) for _ in out_dtypes]
    if kind == "vt":
        out_shape[1] = jax.ShapeDtypeStruct((m // tm, ncols, tm), out_dtypes[1])
        out_specs[1] = pl.BlockSpec((1, tn, tm), lambda j, i: (i, j, 0))
    outs = pl.pallas_call(
        functools.partial(_inproj_kernel, kind=kind, scale=scale),
        out_shape=out_shape,
        grid=(ncols // tn, m // tm),
        in_specs=in_specs,
        out_specs=out_specs,
        scratch_shapes=[pltpu.VMEM((d, tn), BF16)],
        compiler_params=_params(("arbitrary", "arbitrary")),
        name="inproj_" + kind,
    )(h, w, *extra)
    return outs


L_ROWS = 16
LANE_GROUP = 512


def _attn_prompt_kernel(lq1, lk1, lq2, lk2, sg_ref, q_ref, k_ref, vt_ref, o_ref, m_ref, acc_ref,
                        s_ref, *, tq, lam_init):
    qi = pl.program_id(1)
    q = q_ref[...]
    lane = lax.broadcasted_iota(jnp.int32, q.shape, 1)
    zero = jnp.zeros_like(q)
    qb = jnp.concatenate([jnp.where(lane < HEAD_DIM, q, zero),
                          jnp.where(lane >= HEAD_DIM, q, zero)], axis=0)
    ones = jnp.ones((L_ROWS, tq), BF16)
    nt = (((1,), (1,)), ((), ()))

    m_ref[...] = jnp.full(m_ref.shape, NEG_INF, F32)
    acc_ref[...] = jnp.zeros(acc_ref.shape, F32)

    n_groups = 2 * tq // LANE_GROUP
    lanes = [slice(g * LANE_GROUP, (g + 1) * LANE_GROUP) for g in range(n_groups)]

    def scores(ki, g):
        k = k_ref[pl.ds(pl.multiple_of(ki * tq, tq), tq), :]
        return lax.dot_general(k, qb[lanes[g]], nt, preferred_element_type=F32)

    def causal(s, g):
        key = lax.broadcasted_iota(jnp.int32, s.shape, 0)
        qry = lax.broadcasted_iota(jnp.int32, s.shape, 1) + (g * LANE_GROUP) % tq
        return jnp.where(key <= qry, s, NEG_INF)

    def values(ki):
        return jnp.concatenate([vt_ref[ki], ones], axis=0)

    def softmax_pv(s, g, vt):
        ls = lanes[g]
        m_old = m_ref[:, ls]
        m_new = jnp.maximum(m_old, jnp.max(s, axis=0, keepdims=True))
        alpha = jnp.exp2(m_old - m_new)
        p = jnp.exp2(s - m_new).astype(BF16)
        acc_ref[:, ls] = alpha * acc_ref[:, ls] + jnp.dot(vt, p, preferred_element_type=F32)
        m_ref[:, ls] = m_new

    s_ref[...] = scores(0, 0)

    def body(ki, carry):
        vt = values(ki)
        s_first = s_ref[...]
        s_rest = [scores(ki, g) for g in range(1, n_groups)]
        softmax_pv(s_first, 0, vt)
        s_ref[...] = scores(ki + 1, 0)
        for g in range(1, n_groups):
            softmax_pv(s_rest[g - 1], g, vt)
        return carry

    lax.fori_loop(0, qi, body, 0)

    vt = values(qi)
    s_first = causal(s_ref[...], 0)
    s_rest = [causal(scores(qi, g), g) for g in range(1, n_groups)]
    softmax_pv(s_first, 0, vt)
    for g in range(1, n_groups):
        softmax_pv(s_rest[g - 1], g, vt)

    acc = acc_ref[...]
    n = acc[:V_DIM] / acc[V_DIM:V_DIM + 1]
    lam = _lambda(lq1, lk1, lq2, lk2, lam_init)
    att = n[:, :tq] - lam * n[:, tq:]
    y = att * lax.rsqrt(jnp.mean(att * att, axis=0, keepdims=True) + RMS_EPS) * sg_ref[...]
    o_ref[...] = (y * (1.0 - lam_init)).T.astype(o_ref.dtype)


def _attn_prompt(q, k, vt, lams, subln_g_col, lam_init, tq):
    s, w = q.shape
    n_heads = w // V_DIM
    assert vt.shape == (s // tq, w, tq) and (2 * tq) % LANE_GROUP == 0 and 2 * tq >= 2 * LANE_GROUP
    lam_specs = [pl.BlockSpec((1, HEAD_DIM), lambda h, i: (0, 0)) for _ in lams]
    return pl.pallas_call(
        functools.partial(_attn_prompt_kernel, tq=tq, lam_init=lam_init),
        out_shape=jax.ShapeDtypeStruct((s, w), BF16),
        grid=(n_heads, s // tq),
        in_specs=lam_specs + [
            pl.BlockSpec((V_DIM, 1), lambda h, i: (0, 0)),
            pl.BlockSpec((tq, V_DIM), lambda h, i: (i, h)),
            pl.BlockSpec((s, V_DIM), lambda h, i: (0, h)),
            pl.BlockSpec((s // tq, V_DIM, tq), lambda h, i: (0, h, 0)),
        ],
        out_specs=pl.BlockSpec((tq, V_DIM), lambda h, i: (i, h)),
        scratch_shapes=[pltpu.VMEM((1, 2 * tq), F32), pltpu.VMEM((V_DIM + L_ROWS, 2 * tq), F32),
                        pltpu.VMEM((tq, LANE_GROUP), F32)],
        compiler_params=_params(("arbitrary", "arbitrary")),
        name="attn_prompt",
    )(*lams, subln_g_col, q, k, vt)


def _attn_decode_kernel(pt_ref, lq1, lk1, lq2, lk2, sg_ref, q_ref, kn_ref, vn_ref, *refs,
                        pages, t_new, n_heads, lam_init):
    k_refs = refs[:pages]
    v_refs = refs[pages:2 * pages]
    o_ref, m_ref, l_ref, acc_ref = refs[2 * pages:]
    t = pl.program_id(1)
    q = q_ref[0]
    rows_per_head = 2 * t_new
    nt = (((1,), (1,)), ((), ()))

    def own_head(shape):
        row = lax.broadcasted_iota(jnp.int32, shape, 0)
        col = lax.broadcasted_iota(jnp.int32, shape, 1)
        return (col % n_heads) == (row // rows_per_head), row, col

    @pl.when(t == 0)
    def _init_from_new_tokens():
        s = lax.dot_general(q, kn_ref[0], nt, preferred_element_type=F32)
        same, row, col = own_head(s.shape)
        s = jnp.where(same & ((col // n_heads) <= (row % t_new)), s, NEG_INF)
        m = jnp.max(s, axis=1, keepdims=True)
        p = jnp.exp2(s - m)
        m_ref[...] = m
        l_ref[...] = jnp.sum(p, axis=1, keepdims=True)
        acc_ref[...] = jnp.dot(p.astype(BF16), vn_ref[0], preferred_element_type=F32)

    s = jnp.concatenate(
        [lax.dot_general(q, k_refs[i][0].astype(BF16), nt, preferred_element_type=F32)
         for i in range(pages)], axis=1)
    same, _, _ = own_head(s.shape)
    s = jnp.where(same, s, NEG_INF)
    m_old = m_ref[...]
    m_new = jnp.maximum(m_old, jnp.max(s, axis=1, keepdims=True))
    alpha = jnp.exp2(m_old - m_new)
    p = jnp.exp2(s - m_new)
    l_ref[...] = alpha * l_ref[...] + jnp.sum(p, axis=1, keepdims=True)
    pb = p.astype(BF16)
    w = k_refs[0].shape[1]
    pv = jnp.dot(pb[:, :w], v_refs[0][0].astype(BF16), preferred_element_type=F32)
    for i in range(1, pages):
        pv += jnp.dot(pb[:, i * w:(i + 1) * w], v_refs[i][0].astype(BF16),
                      preferred_element_type=F32)
    acc_ref[...] = alpha * acc_ref[...] + pv
    m_ref[...] = m_new

    @pl.when(t == pl.num_programs(1) - 1)
    def _finalize():
        lam = _lambda(lq1, lk1, lq2, lk2, lam_init)
        n = acc_ref[...] / l_ref[...]
        for h in range(n_heads):
            r0 = h * rows_per_head
            att = n[r0:r0 + t_new] - lam * n[r0 + t_new:r0 + rows_per_head]
            o_ref[0, :, h * V_DIM:(h + 1) * V_DIM] = _rms(att, sg_ref[...]) * (1.0 - lam_init)


def _attn_decode(q, k_new, v_new, cache_k, cache_v, page_table, lams, subln_g, lam_init, t_new,
                 n_heads, pages):
    bs, r, _ = q.shape
    n_pages = page_table.shape[1]
    assert n_pages % pages == 0
    pt_flat = page_table.reshape(-1)
    page_rows = cache_k.shape[1]

    def cache_spec(i):
        return pl.BlockSpec((1, page_rows, V_DIM),
                            lambda b, t, pt: (pt[b * n_pages + t * pages + i], 0, 0))

    const = lambda b, t, pt: (0, 0)
    per_seq = lambda b, t, pt: (b, 0, 0)
    grid_spec = pltpu.PrefetchScalarGridSpec(
        num_scalar_prefetch=1,
        grid=(bs, n_pages // pages),
        in_specs=[pl.BlockSpec((1, HEAD_DIM), const) for _ in lams] + [
            pl.BlockSpec((1, V_DIM), const),
            pl.BlockSpec((1, r, V_DIM), per_seq),
            pl.BlockSpec((1,) + k_new.shape[1:], per_seq),
            pl.BlockSpec((1,) + v_new.shape[1:], per_seq),
        ] + [cache_spec(i) for i in range(pages)] + [cache_spec(i) for i in range(pages)],
        out_specs=pl.BlockSpec((1, t_new, n_heads * V_DIM), per_seq),
        scratch_shapes=[pltpu.VMEM((r, 1), F32), pltpu.VMEM((r, 1), F32), pltpu.VMEM((r, V_DIM), F32)],
    )
    return pl.pallas_call(
        functools.partial(_attn_decode_kernel, pages=pages, t_new=t_new, n_heads=n_heads,
                          lam_init=lam_init),
        out_shape=jax.ShapeDtypeStruct((bs, t_new, n_heads * V_DIM), F32),
        grid_spec=grid_spec,
        compiler_params=_params(("arbitrary", "arbitrary")),
        name="attn_decode",
    )(pt_flat, *lams, subln_g, q, k_new, v_new, *([cache_k] * pages), *([cache_v] * pages))


def _gating_kernel(u_ref, vg_ref, w_ref, b_ref, o_ref, *, chunk_len):
    tm = u_ref.shape[0]
    n_groups = w_ref.shape[0]
    row = lax.broadcasted_iota(jnp.int32, (CHUNK, CHUNK), 0)
    col = lax.broadcasted_iota(jnp.int32, (CHUNK, CHUNK), 1)
    mask = (row // chunk_len == col // chunk_len) & (col <= row)
    for g in range(n_groups):
        wg = jnp.where(mask, w_ref[g], 0.0).astype(BF16)
        bg = b_ref[g]
        cs = slice(g * CHUNK, (g + 1) * CHUNK)
        for r in range(tm // CHUNK):
            rs = slice(r * CHUNK, (r + 1) * CHUNK)
            sg = jnp.dot(wg, vg_ref[rs, cs].astype(BF16), preferred_element_type=F32) + bg
            o_ref[rs, cs] = (u_ref[rs, cs] * sg).astype(o_ref.dtype)


def _gating(u, vg, wsp, bsp, chunk_len, tm):
    m, d = u.shape
    g = wsp.shape[0]
    return pl.pallas_call(
        functools.partial(_gating_kernel, chunk_len=chunk_len),
        out_shape=jax.ShapeDtypeStruct((m, d), BF16),
        grid=(m // tm,),
        in_specs=[pl.BlockSpec((tm, d), lambda i: (i, 0)),
                  pl.BlockSpec((tm, d), lambda i: (i, 0)),
                  pl.BlockSpec((g, CHUNK, CHUNK), lambda i: (0, 0, 0)),
                  pl.BlockSpec((g, CHUNK, 1), lambda i: (0, 0, 0))],
        out_specs=pl.BlockSpec((tm, d), lambda i: (i, 0)),
        compiler_params=_params(("arbitrary",)),
        name="spatial_gating",
    )(u, vg, wsp, bsp)


def _branch_kernel(att_ref, ug_ref, ga_ref, gb_ref, wa_ref, wb_ref, m_ref):
    a = jnp.dot(att_ref[...].astype(BF16), wa_ref[...], preferred_element_type=F32)
    g = jnp.dot(ug_ref[...], wb_ref[...], preferred_element_type=F32)
    m_ref[...] = (ga_ref[...] * a + gb_ref[...] * g).astype(m_ref.dtype)


def _branch(att, ug, gates, wa, wb, tm, tn=1024):
    m, d_in = att.shape
    d = wa.shape[1]
    nj = d // tn
    return pl.pallas_call(
        _branch_kernel,
        out_shape=jax.ShapeDtypeStruct((m, d), BF16),
        grid=(nj, m // tm),
        in_specs=[pl.BlockSpec((tm, d_in), lambda j, i: (i, 0)),
                  pl.BlockSpec((tm, d_in), lambda j, i: (i, 0)),
                  pl.BlockSpec((tm, tn), lambda j, i: (i, j)),
                  pl.BlockSpec((tm, tn), lambda j, i: (i, nj + j)),
                  pl.BlockSpec((d_in, tn), lambda j, i: (0, j)),
                  pl.BlockSpec((d_in, tn), lambda j, i: (0, j))],
        out_specs=pl.BlockSpec((tm, tn), lambda j, i: (i, j)),
        compiler_params=_params(("arbitrary", "arbitrary")),
        name="branch_merge",
    )(att, ug, gates, gates, wa, wb)


def _outproj_kernel(m_ref, x_ref, w_ref, g_ref, y_ref, h_ref):
    y = x_ref[...] + jnp.dot(m_ref[...], w_ref[...], preferred_element_type=F32)
    y_ref[...] = y
    h_ref[...] = _rms(y, g_ref[...]).astype(h_ref.dtype)


def _outproj(mm, x, w, g, tm):
    m, d = x.shape
    return pl.pallas_call(
        _outproj_kernel,
        out_shape=[jax.ShapeDtypeStruct((m, d), F32), jax.ShapeDtypeStruct((m, d), BF16)],
        grid=(m // tm,),
        in_specs=[pl.BlockSpec((tm, d), lambda i: (i, 0)),
                  pl.BlockSpec((tm, d), lambda i: (i, 0)),
                  pl.BlockSpec((d, d), lambda i: (0, 0)),
                  pl.BlockSpec((1, d), lambda i: (0, 0))],
        out_specs=[pl.BlockSpec((tm, d), lambda i: (i, 0)),
                   pl.BlockSpec((tm, d), lambda i: (i, 0))],
        compiler_params=_params(("arbitrary",)),
        name="outproj",
    )(mm, x, w, g)


def _ffn_kernel(h_ref, y_ref, w1_ref, w2_ref, g_ref, o_ref):
    f = pl.program_id(1)

    @pl.when(f == 0)
    def _init():
        o_ref[...] = y_ref[...]

    h = h_ref[...]
    half = w1_ref.shape[1] // 2
    a = [jnp.dot(h, w1_ref[:, c * half:(c + 1) * half], preferred_element_type=F32) for c in range(2)]
    part = None
    for c in range(2):
        r = jnp.square(jnp.maximum(a[c], 0.0)).astype(BF16)
        d = jnp.dot(r, w2_ref[c * half:(c + 1) * half, :], preferred_element_type=F32)
        part = d if part is None else part + d
    o_ref[...] += part

    @pl.when(f == pl.num_programs(1) - 1)
    def _final_norm():
        o_ref[...] = _rms(o_ref[...], g_ref[...])


def _ffn(h, y, w1, w2, g, tm, tf=1024):
    m, d = h.shape
    d_ff = w1.shape[1]
    return pl.pallas_call(
        _ffn_kernel,
        out_shape=jax.ShapeDtypeStruct((m, d), F32),
        grid=(m // tm, d_ff // tf),
        in_specs=[pl.BlockSpec((tm, d), lambda i, f: (i, 0)),
                  pl.BlockSpec((tm, d), lambda i, f: (i, 0)),
                  pl.BlockSpec((d, tf), lambda i, f: (0, f)),
                  pl.BlockSpec((tf, d), lambda i, f: (f, 0)),
                  pl.BlockSpec((1, d), lambda i, f: (0, 0))],
        out_specs=pl.BlockSpec((tm, d), lambda i, f: (i, 0)),
        compiler_params=_params(("arbitrary", "arbitrary")),
        name="ffn",
    )(h, y, w1, w2, g)


def _token_path_in(x, norm_g, w_in, ln_g, ln_b, d_attn, d_gmlp, tm, vg_dtype, v_kind):
    d_model = x.shape[1]
    h = _rmsnorm(x, norm_g, tm)
    c = 0
    (q,) = _inproj(h, w_in, c, d_attn, "q", [BF16], tm, scale=HEAD_DIM ** -0.5 * LOG2_E)
    c += d_attn
    k32, k16 = _inproj(h, w_in, c, d_attn, "kv", [F32, BF16], tm)
    c += d_attn
    v32, v16 = _inproj(h, w_in, c, d_attn, v_kind, [F32, BF16], tm)
    c += d_attn
    (u,) = _inproj(h, w_in, c, d_gmlp, "u", [F32], tm)
    c += d_gmlp
    (vg,) = _inproj(h, w_in, c, d_gmlp, "vs", [vg_dtype], tm, extra=(ln_g, ln_b))
    c += d_gmlp
    (gates,) = _inproj(h, w_in, c, 2 * d_model, "gates", [F32], tm)
    return q, k32, k16, v32, v16, u, vg, gates


def _token_path_out(x, att, u, vg, gates, wsp, bsp, chunk_len, wa, wb, wo, norm_ffn_g, w1, w2,
                    norm_final_g, tm):
    ug = _gating(u, vg, wsp, bsp, chunk_len, tm)
    mm = _branch(att, ug, gates, wa, wb, tm)
    y, h2 = _outproj(mm, x, wo, norm_ffn_g, tm)
    return _ffn(h2, y, w1, w2, norm_final_g, tm)


def kernel(x_prompt, x_sample, cache_k, cache_v, page_table, norm_mix_g, w_in, lambda_q1, lambda_k1,
           lambda_q2, lambda_k2, subln_g, gmlp_ln_g, gmlp_ln_b, w_spatial, b_spatial, w_branch_a,
           w_branch_b, w_out, norm_ffn_g, w_ff1, w_ff2, norm_final_g):
    bp, sp, d_model = x_prompt.shape
    bs, ts, _ = x_sample.shape
    depth = w_in.shape[0]
    assert bp == 1 and depth == 1, "the final norm is fused into the MLP kernel of the single layer"
    n_heads = cache_k.shape[3]
    d_attn = n_heads * V_DIM
    d_gmlp = gmlp_ln_g.shape[1]
    n_groups = w_spatial.shape[1]
    page = cache_k.shape[2]
    assert sp % CHUNK == 0 and CHUNK % ts == 0 and (bs * ts) % CHUNK == 0

    l = 0
    lam_init = 0.8 - 0.6 * math.exp(-0.3 * l)
    row2 = lambda a: a.reshape(1, -1)
    lams = [row2(lambda_q1[l]), row2(lambda_k1[l]), row2(lambda_q2[l]), row2(lambda_k2[l])]
    sub_g = row2(subln_g[l])
    ln_g, ln_b = row2(gmlp_ln_g[l]), row2(gmlp_ln_b[l])
    mix_g, ffn_g, fin_g = row2(norm_mix_g[l]), row2(norm_ffn_g[l]), row2(norm_final_g)
    wa, wb, wo = (w_branch_a[l].astype(BF16), w_branch_b[l].astype(BF16), w_out[l].astype(BF16))
    w1, w2 = w_ff1[l].astype(BF16), w_ff2[l].astype(BF16)

    xp = x_prompt.reshape(sp, d_model)
    tm_p = _row_tile(sp, 512)
    q, k32, k16, v32, v16t, u, vg, gates = _token_path_in(
        xp, mix_g, w_in[l], ln_g, ln_b, d_attn, d_gmlp, tm_p, BF16, "vt")
    att = _attn_prompt(q, k16, v16t, lams, subln_g[l].reshape(-1, 1), lam_init, tm_p)
    bsp_p = b_spatial[l].reshape(n_groups, CHUNK, 1)
    yp = _token_path_out(xp, att, u, vg, gates, w_spatial[l], bsp_p, CHUNK, wa, wb, wo, ffn_g,
                         w1, w2, fin_g, tm_p)
    new_k_prompt = k32.reshape(1, bp, sp, n_heads, V_DIM)
    new_v_prompt = v32.reshape(1, bp, sp, n_heads, V_DIM)

    ms = bs * ts
    xs = x_sample.reshape(ms, d_model)
    tm_s = _row_tile(ms, 512)
    q, k32, k16, v32, v16, u, vg, gates = _token_path_in(
        xs, mix_g, w_in[l], ln_g, ln_b, d_attn, d_gmlp, tm_s, F32, "kv")
    q5 = q.reshape(bs, ts, n_heads, 2, HEAD_DIM).transpose(0, 2, 3, 1, 4)
    eye = jnp.eye(2, dtype=BF16)
    q_rows = (q5[:, :, :, :, None, :] * eye[None, None, :, None, :, None]).reshape(
        bs, n_heads * 2 * ts, V_DIM)
    k_new = k16.reshape(bs, ts * n_heads, V_DIM)
    v_new = v16.reshape(bs, ts * n_heads, V_DIM)
    ck = cache_k[l].reshape(-1, page * n_heads, V_DIM)
    cv = cache_v[l].reshape(-1, page * n_heads, V_DIM)
    n_pages = page_table.shape[1]
    pages = next(p for p in (16, 8, 4, 2, 1) if n_pages % p == 0)
    att_s = _attn_decode(q_rows, k_new, v_new, ck, cv, page_table, lams, sub_g, lam_init, ts,
                         n_heads, pages)
    att_s = att_s.reshape(ms, d_attn)
    reps = CHUNK // ts
    wsp_s = jnp.tile(w_spatial[l][:, :ts, :ts], (1, reps, reps))
    bsp_s = jnp.tile(b_spatial[l][:, :ts], (1, reps)).reshape(n_groups, CHUNK, 1)
    ys = _token_path_out(xs, att_s, u, vg, gates, wsp_s, bsp_s, ts, wa, wb, wo, ffn_g,
                         w1, w2, fin_g, tm_s)

    return (yp.reshape(bp, sp, d_model), ys.reshape(bs, ts, d_model), new_k_prompt, new_v_prompt,
            k32.reshape(1, bs, ts, n_heads, V_DIM), v32.reshape(1, bs, ts, n_heads, V_DIM),
            vg.reshape(1, bs, ts, d_gmlp))
```

```python
import functools
import math

import jax
import jax.numpy as jnp
from jax import lax
from jax.experimental import pallas as pl
from jax.experimental.pallas import tpu as pltpu

F32 = jnp.float32
BF16 = jnp.bfloat16

HEAD_DIM = 64
V_DIM = 2 * HEAD_DIM
CHUNK = 128
NEG_INF = -1e30
LOG2_E = math.log2(math.e)
RMS_EPS = 1e-6
LN_EPS = 1e-5
MIB = 1024 * 1024
V7X_VMEM_BUDGET = 56 * MIB
TM_INPROJ = 1024
TM_MIXER = 256
TM_FFN = 512


def _row_tile(m, target):
    t = min(m, target)
    assert m % t == 0
    return t


def _params(sem, vmem=V7X_VMEM_BUDGET):
    return pltpu.CompilerParams(dimension_semantics=sem, vmem_limit_bytes=vmem)


def _rms(x, g):
    return x * lax.rsqrt(jnp.mean(x * x, axis=-1, keepdims=True) + RMS_EPS) * g


def _lambda(lq1, lk1, lq2, lk2, lam_init):
    a = jnp.exp(jnp.sum(lq1[...] * lk1[...], axis=1, keepdims=True))
    b = jnp.exp(jnp.sum(lq2[...] * lk2[...], axis=1, keepdims=True))
    return a - b + lam_init


def _rmsnorm_kernel(x_ref, g_ref, o_ref):
    o_ref[...] = _rms(x_ref[...], g_ref[...]).astype(o_ref.dtype)


def _rmsnorm(x, g, tm):
    m, d = x.shape
    return pl.pallas_call(
        _rmsnorm_kernel,
        out_shape=jax.ShapeDtypeStruct((m, d), BF16),
        grid=(m // tm,),
        in_specs=[pl.BlockSpec((tm, d), lambda i: (i, 0)),
                  pl.BlockSpec((1, d), lambda i: (0, 0))],
        out_specs=pl.BlockSpec((tm, d), lambda i: (i, 0)),
        compiler_params=_params(("arbitrary",)),
        name="rmsnorm",
    )(x, g)


def _inproj_kernel(h_ref, w_ref, *refs, kind, scale):
    wbf_ref = refs[-1]

    @pl.when(pl.program_id(1) == 0)
    def _cast_weights():
        wbf_ref[...] = w_ref[...].astype(BF16)

    z = jnp.dot(h_ref[...], wbf_ref[...], preferred_element_type=F32)
    if kind == "q":
        refs[0][...] = (z * scale).astype(BF16)
    elif kind == "kv":
        refs[0][...] = z
        refs[1][...] = z.astype(BF16)
    elif kind == "vt":
        refs[0][...] = z
        tk = refs[1].shape[2]
        for c in range(refs[1].shape[0]):
            refs[1][c] = z[c * tk:(c + 1) * tk].T.astype(BF16)
    elif kind == "u":
        refs[0][...] = jax.nn.gelu(z).astype(refs[0].dtype)
    elif kind == "vs":
        g_ref, b_ref, o_ref = refs[0], refs[1], refs[2]
        y = jax.nn.gelu(z)
        mu = jnp.mean(y, axis=-1, keepdims=True)
        yc = y - mu
        var = jnp.mean(yc * yc, axis=-1, keepdims=True)
        o_ref[...] = (yc * lax.rsqrt(var + LN_EPS) * g_ref[...] + b_ref[...]).astype(o_ref.dtype)
    elif kind == "gates":
        refs[0][...] = jax.nn.sigmoid(z).astype(refs[0].dtype)
    else:
        raise ValueError(kind)


def _inproj(h, w, col0, ncols, kind, out_dtypes, tm, extra=(), scale=1.0, tn=1024):
    m, d = h.shape
    assert col0 % tn == 0 and ncols % tn == 0
    j0 = col0 // tn
    in_specs = [pl.BlockSpec((tm, d), lambda j, i: (i, 0)),
                pl.BlockSpec((d, tn), lambda j, i: (0, j0 + j))]
    in_specs += [pl.BlockSpec((1, tn), lambda j, i: (0, 0)) for _ in extra]
    out_shape = [jax.ShapeDtypeStruct((m, ncols), dt) for dt in out_dtypes]
    out_specs = [pl.BlockSpec((tm, tn), lambda j, i: (i, j)) for _ in out_dtypes]
    if kind == "vt":
        tk = min(tm, ATTN_TK)
        out_shape[1] = jax.ShapeDtypeStruct((m // tk, ncols, tk), out_dtypes[1])
        out_specs[1] = pl.BlockSpec((tm // tk, tn, tk), lambda j, i: (i, j, 0))
    outs = pl.pallas_call(
        functools.partial(_inproj_kernel, kind=kind, scale=scale),
        out_shape=out_shape,
        grid=(ncols // tn, m // tm),
        in_specs=in_specs,
        out_specs=out_specs,
        scratch_shapes=[pltpu.VMEM((d, tn), BF16)],
        compiler_params=_params(("arbitrary", "arbitrary")),
        name="inproj_" + kind,
    )(h, w, *extra)
    return outs


ATTN_TK = 512
ATTN_TQ = 1024
L_ROWS = 16
LANE_GROUP = 512


def _attn_prompt_kernel(lq1, lk1, lq2, lk2, sg_ref, q_ref, k_ref, vt_ref, o_ref, m_ref, acc_ref,
                        s_ref, *, tq, tk, lam_init):
    qi = pl.program_id(1)
    q = q_ref[...]
    lane = lax.broadcasted_iota(jnp.int32, q.shape, 1)
    zero = jnp.zeros_like(q)
    qb = jnp.concatenate([jnp.where(lane < HEAD_DIM, q, zero),
                          jnp.where(lane >= HEAD_DIM, q, zero)], axis=0)
    ones = jnp.ones((L_ROWS, tk), BF16)
    nt = (((1,), (1,)), ((), ()))

    m_ref[...] = jnp.full(m_ref.shape, NEG_INF, F32)
    acc_ref[...] = jnp.zeros(acc_ref.shape, F32)

    n_groups = 2 * tq // LANE_GROUP
    lanes = [slice(g * LANE_GROUP, (g + 1) * LANE_GROUP) for g in range(n_groups)]
    n_full = qi * (tq // tk)

    def scores(ki, g):
        k = k_ref[pl.ds(pl.multiple_of(ki * tk, tk), tk), :]
        return lax.dot_general(k, qb[lanes[g]], nt, preferred_element_type=F32)

    def first_query(g):
        return (g * LANE_GROUP) % tq

    def causal(s, g, j):
        key = lax.broadcasted_iota(jnp.int32, s.shape, 0) + j * tk
        qry = lax.broadcasted_iota(jnp.int32, s.shape, 1) + first_query(g)
        return jnp.where(key <= qry, s, NEG_INF)

    def values(ki):
        return jnp.concatenate([vt_ref[ki], ones], axis=0)

    def softmax_pv(s, g, vt):
        ls = lanes[g]
        m_old = m_ref[:, ls]
        m_new = jnp.maximum(m_old, jnp.max(s, axis=0, keepdims=True))
        alpha = jnp.exp2(m_old - m_new)
        p = jnp.exp2(s - m_new).astype(BF16)
        acc_ref[:, ls] = alpha * acc_ref[:, ls] + jnp.dot(vt, p, preferred_element_type=F32)
        m_ref[:, ls] = m_new

    s_ref[...] = scores(0, 0)

    def body(ki, carry):
        vt = values(ki)
        s = s_ref[...]
        for g in range(n_groups):
            if g + 1 < n_groups:
                s_next = scores(ki, g + 1)
            else:
                s_ref[...] = scores(ki + 1, 0)
            softmax_pv(s, g, vt)
            s = s_next
        return carry

    lax.fori_loop(0, n_full, body, 0)

    for j in range(tq // tk):
        ki = n_full + j
        vt = values(ki)
        active = [g for g in range(n_groups) if first_query(g) + LANE_GROUP - 1 >= j * tk]

        def masked_scores(g):
            s = s_ref[...] if (j == 0 and g == 0) else scores(ki, g)
            fully_visible = first_query(g) >= (j + 1) * tk - 1
            return s if fully_visible else causal(s, g, j)

        s = masked_scores(active[0])
        for idx, g in enumerate(active):
            if idx + 1 < len(active):
                s_next = masked_scores(active[idx + 1])
            softmax_pv(s, g, vt)
            s = s_next

    acc = acc_ref[...]
    n = acc[:V_DIM] / acc[V_DIM:V_DIM + 1]
    lam = _lambda(lq1, lk1, lq2, lk2, lam_init)
    att = n[:, :tq] - lam * n[:, tq:]
    y = att * lax.rsqrt(jnp.mean(att * att, axis=0, keepdims=True) + RMS_EPS) * sg_ref[...]
    o_ref[...] = (y * (1.0 - lam_init)).T.astype(o_ref.dtype)


def _attn_prompt(q, k, vt, lams, subln_g_col, lam_init, tq):
    s, w = q.shape
    n_heads = w // V_DIM
    tk = vt.shape[2]
    assert vt.shape == (s // tk, w, tk) and tq % tk == 0 and s % tq == 0
    assert (2 * tq) % LANE_GROUP == 0 and 2 * tq >= 2 * LANE_GROUP and tq % LANE_GROUP == 0
    lam_specs = [pl.BlockSpec((1, HEAD_DIM), lambda h, i: (0, 0)) for _ in lams]
    return pl.pallas_call(
        functools.partial(_attn_prompt_kernel, tq=tq, tk=tk, lam_init=lam_init),
        out_shape=jax.ShapeDtypeStruct((s, w), BF16),
        grid=(n_heads, s // tq),
        in_specs=lam_specs + [
            pl.BlockSpec((V_DIM, 1), lambda h, i: (0, 0)),
            pl.BlockSpec((tq, V_DIM), lambda h, i: (i, h)),
            pl.BlockSpec((s, V_DIM), lambda h, i: (0, h)),
            pl.BlockSpec((s // tk, V_DIM, tk), lambda h, i: (0, h, 0)),
        ],
        out_specs=pl.BlockSpec((tq, V_DIM), lambda h, i: (i, h)),
        scratch_shapes=[pltpu.VMEM((1, 2 * tq), F32), pltpu.VMEM((V_DIM + L_ROWS, 2 * tq), F32),
                        pltpu.VMEM((tk, LANE_GROUP), F32)],
        compiler_params=_params(("arbitrary", "arbitrary")),
        name="attn_prompt",
    )(*lams, subln_g_col, q, k, vt)


def _attn_decode_kernel(pt_ref, lq1, lk1, lq2, lk2, sg_ref, q_ref, kn_ref, vn_ref, *refs,
                        pages, t_new, n_heads, lam_init):
    k_refs = refs[:pages]
    v_refs = refs[pages:2 * pages]
    o_ref, m_ref, l_ref, acc_ref = refs[2 * pages:]
    t = pl.program_id(1)
    q = q_ref[0]
    rows_per_head = 2 * t_new
    nt = (((1,), (1,)), ((), ()))

    def own_head(shape):
        row = lax.broadcasted_iota(jnp.int32, shape, 0)
        col = lax.broadcasted_iota(jnp.int32, shape, 1)
        return (col % n_heads) == (row // rows_per_head), row, col

    @pl.when(t == 0)
    def _init_from_new_tokens():
        s = lax.dot_general(q, kn_ref[0], nt, preferred_element_type=F32)
        same, row, col = own_head(s.shape)
        s = jnp.where(same & ((col // n_heads) <= (row % t_new)), s, NEG_INF)
        m = jnp.max(s, axis=1, keepdims=True)
        p = jnp.exp2(s - m)
        m_ref[...] = m
        l_ref[...] = jnp.sum(p, axis=1, keepdims=True)
        acc_ref[...] = jnp.dot(p.astype(BF16), vn_ref[0], preferred_element_type=F32)

    s = jnp.concatenate(
        [lax.dot_general(q, k_refs[i][0].astype(BF16), nt, preferred_element_type=F32)
         for i in range(pages)], axis=1)
    same, _, _ = own_head(s.shape)
    s = jnp.where(same, s, NEG_INF)
    m_old = m_ref[...]
    m_new = jnp.maximum(m_old, jnp.max(s, axis=1, keepdims=True))
    alpha = jnp.exp2(m_old - m_new)
    p = jnp.exp2(s - m_new)
    l_ref[...] = alpha * l_ref[...] + jnp.sum(p, axis=1, keepdims=True)
    pb = p.astype(BF16)
    w = k_refs[0].shape[1]
    pv = jnp.dot(pb[:, :w], v_refs[0][0].astype(BF16), preferred_element_type=F32)
    for i in range(1, pages):
        pv += jnp.dot(pb[:, i * w:(i + 1) * w], v_refs[i][0].astype(BF16),
                      preferred_element_type=F32)
    acc_ref[...] = alpha * acc_ref[...] + pv
    m_ref[...] = m_new

    @pl.when(t == pl.num_programs(1) - 1)
    def _finalize():
        lam = _lambda(lq1, lk1, lq2, lk2, lam_init)
        n = acc_ref[...] / l_ref[...]
        for h in range(n_heads):
            r0 = h * rows_per_head
            att = n[r0:r0 + t_new] - lam * n[r0 + t_new:r0 + rows_per_head]
            o_ref[0, :, h * V_DIM:(h + 1) * V_DIM] = _rms(att, sg_ref[...]) * (1.0 - lam_init)


def _attn_decode(q, k_new, v_new, cache_k, cache_v, page_table, lams, subln_g, lam_init, t_new,
                 n_heads, pages):
    bs, r, _ = q.shape
    n_pages = page_table.shape[1]
    assert n_pages % pages == 0
    pt_flat = page_table.reshape(-1)
    page_rows = cache_k.shape[1]

    def cache_spec(i):
        return pl.BlockSpec((1, page_rows, V_DIM),
                            lambda b, t, pt: (pt[b * n_pages + t * pages + i], 0, 0))

    const = lambda b, t, pt: (0, 0)
    per_seq = lambda b, t, pt: (b, 0, 0)
    grid_spec = pltpu.PrefetchScalarGridSpec(
        num_scalar_prefetch=1,
        grid=(bs, n_pages // pages),
        in_specs=[pl.BlockSpec((1, HEAD_DIM), const) for _ in lams] + [
            pl.BlockSpec((1, V_DIM), const),
            pl.BlockSpec((1, r, V_DIM), per_seq),
            pl.BlockSpec((1,) + k_new.shape[1:], per_seq),
            pl.BlockSpec((1,) + v_new.shape[1:], per_seq),
        ] + [cache_spec(i) for i in range(pages)] + [cache_spec(i) for i in range(pages)],
        out_specs=pl.BlockSpec((1, t_new, n_heads * V_DIM), per_seq),
        scratch_shapes=[pltpu.VMEM((r, 1), F32), pltpu.VMEM((r, 1), F32), pltpu.VMEM((r, V_DIM), F32)],
    )
    return pl.pallas_call(
        functools.partial(_attn_decode_kernel, pages=pages, t_new=t_new, n_heads=n_heads,
                          lam_init=lam_init),
        out_shape=jax.ShapeDtypeStruct((bs, t_new, n_heads * V_DIM), F32),
        grid_spec=grid_spec,
        compiler_params=_params(("arbitrary", "arbitrary")),
        name="attn_decode",
    )(pt_flat, *lams, subln_g, q, k_new, v_new, *([cache_k] * pages), *([cache_v] * pages))


def _mixer_out_kernel(att_ref, u_ref, vg_ref, ga_ref, gb_ref, x_ref, wsp_ref, bsp_ref, wa_ref, wb_ref,
                      wo_ref, g_ref, y_ref, h_ref, ug_ref, *, chunk_len):
    tm = u_ref.shape[0]
    n_groups = wsp_ref.shape[0]
    row = lax.broadcasted_iota(jnp.int32, (CHUNK, CHUNK), 0)
    col = lax.broadcasted_iota(jnp.int32, (CHUNK, CHUNK), 1)
    mask = (row // chunk_len == col // chunk_len) & (col <= row)
    for g in range(n_groups):
        wg = jnp.where(mask, wsp_ref[g], 0.0).astype(BF16)
        bg = bsp_ref[g]
        cs = slice(g * CHUNK, (g + 1) * CHUNK)
        for r in range(tm // CHUNK):
            rs = slice(r * CHUNK, (r + 1) * CHUNK)
            sg = jnp.dot(wg, vg_ref[rs, cs].astype(BF16), preferred_element_type=F32) + bg
            ug_ref[rs, cs] = (u_ref[rs, cs].astype(F32) * sg).astype(BF16)
    a = jnp.dot(att_ref[...].astype(BF16), wa_ref[...], preferred_element_type=F32)
    gp = jnp.dot(ug_ref[...], wb_ref[...], preferred_element_type=F32)
    mm = (ga_ref[...].astype(F32) * a + gb_ref[...].astype(F32) * gp).astype(BF16)
    y = x_ref[...] + jnp.dot(mm, wo_ref[...], preferred_element_type=F32)
    y_ref[...] = y
    h_ref[...] = _rms(y, g_ref[...]).astype(h_ref.dtype)


def _mixer_out(att, u, vg, gates, x, wsp, bsp, wa, wb, wo, g, chunk_len, tm):
    m, d = x.shape
    d_in = att.shape[1]
    ng = wsp.shape[0]
    rows = lambda w: pl.BlockSpec((tm, w), lambda i: (i, 0))
    resident = lambda shape: pl.BlockSpec(shape, lambda i: (0,) * len(shape),
                                          pipeline_mode=pl.Buffered(1))
    return pl.pallas_call(
        functools.partial(_mixer_out_kernel, chunk_len=chunk_len),
        out_shape=[jax.ShapeDtypeStruct((m, d), F32), jax.ShapeDtypeStruct((m, d), BF16)],
        grid=(m // tm,),
        in_specs=[rows(d_in), rows(d_in), rows(d_in),
                  pl.BlockSpec((tm, d), lambda i: (i, 0)),
                  pl.BlockSpec((tm, d), lambda i: (i, 1)),
                  rows(d),
                  resident((ng, CHUNK, CHUNK)), resident((ng, CHUNK, 1)),
                  resident((d_in, d)), resident((d_in, d)), resident((d, d)), resident((1, d))],
        out_specs=[rows(d), rows(d)],
        scratch_shapes=[pltpu.VMEM((tm, d_in), BF16)],
        compiler_params=_params(("arbitrary",)),
        name="mixer_out",
    )(att, u, vg, gates, gates, x, wsp, bsp, wa, wb, wo, g)


def _ffn_kernel(h_ref, y_ref, w1_ref, w2_ref, g_ref, o_ref):
    f = pl.program_id(1)

    @pl.when(f == 0)
    def _init():
        o_ref[...] = y_ref[...]

    h = h_ref[...]
    half = w1_ref.shape[1] // 2
    a = [jnp.dot(h, w1_ref[:, c * half:(c + 1) * half], preferred_element_type=F32) for c in range(2)]
    part = None
    for c in range(2):
        r = jnp.square(jnp.maximum(a[c], 0.0)).astype(BF16)
        d = jnp.dot(r, w2_ref[c * half:(c + 1) * half, :], preferred_element_type=F32)
        part = d if part is None else part + d
    o_ref[...] += part

    @pl.when(f == pl.num_programs(1) - 1)
    def _final_norm():
        o_ref[...] = _rms(o_ref[...], g_ref[...])


def _ffn(h, y, w1, w2, g, tm, tf=1024):
    m, d = h.shape
    d_ff = w1.shape[1]
    return pl.pallas_call(
        _ffn_kernel,
        out_shape=jax.ShapeDtypeStruct((m, d), F32),
        grid=(m // tm, d_ff // tf),
        in_specs=[pl.BlockSpec((tm, d), lambda i, f: (i, 0)),
                  pl.BlockSpec((tm, d), lambda i, f: (i, 0)),
                  pl.BlockSpec((d, tf), lambda i, f: (0, f)),
                  pl.BlockSpec((tf, d), lambda i, f: (f, 0)),
                  pl.BlockSpec((1, d), lambda i, f: (0, 0))],
        out_specs=pl.BlockSpec((tm, d), lambda i, f: (i, 0)),
        compiler_params=_params(("arbitrary", "arbitrary")),
        name="ffn",
    )(h, y, w1, w2, g)


def _token_path_in(x, norm_g, w_in, ln_g, ln_b, d_attn, d_gmlp, tm, vg_dtype, v_kind):
    d_model = x.shape[1]
    h = _rmsnorm(x, norm_g, tm)
    c = 0
    (q,) = _inproj(h, w_in, c, d_attn, "q", [BF16], tm, scale=HEAD_DIM ** -0.5 * LOG2_E)
    c += d_attn
    k32, k16 = _inproj(h, w_in, c, d_attn, "kv", [F32, BF16], tm)
    c += d_attn
    v32, v16 = _inproj(h, w_in, c, d_attn, v_kind, [F32, BF16], tm)
    c += d_attn
    (u,) = _inproj(h, w_in, c, d_gmlp, "u", [BF16], tm)
    c += d_gmlp
    (vg,) = _inproj(h, w_in, c, d_gmlp, "vs", [vg_dtype], tm, extra=(ln_g, ln_b))
    c += d_gmlp
    (gates,) = _inproj(h, w_in, c, 2 * d_model, "gates", [BF16], tm)
    return q, k32, k16, v32, v16, u, vg, gates


def _token_path_out(x, att, u, vg, gates, wsp, bsp, chunk_len, wa, wb, wo, norm_ffn_g, w1, w2,
                    norm_final_g, tm_mix, tm_ffn):
    y, h2 = _mixer_out(att, u, vg, gates, x, wsp, bsp, wa, wb, wo, norm_ffn_g, chunk_len, tm_mix)
    return _ffn(h2, y, w1, w2, norm_final_g, tm_ffn)


def kernel(x_prompt, x_sample, cache_k, cache_v, page_table, norm_mix_g, w_in, lambda_q1, lambda_k1,
           lambda_q2, lambda_k2, subln_g, gmlp_ln_g, gmlp_ln_b, w_spatial, b_spatial, w_branch_a,
           w_branch_b, w_out, norm_ffn_g, w_ff1, w_ff2, norm_final_g):
    bp, sp, d_model = x_prompt.shape
    bs, ts, _ = x_sample.shape
    depth = w_in.shape[0]
    assert bp == 1 and depth == 1, "the final norm is fused into the MLP kernel of the single layer"
    n_heads = cache_k.shape[3]
    d_attn = n_heads * V_DIM
    d_gmlp = gmlp_ln_g.shape[1]
    n_groups = w_spatial.shape[1]
    page = cache_k.shape[2]
    assert sp % CHUNK == 0 and CHUNK % ts == 0 and (bs * ts) % CHUNK == 0

    l = 0
    lam_init = 0.8 - 0.6 * math.exp(-0.3 * l)
    row2 = lambda a: a.reshape(1, -1)
    lams = [row2(lambda_q1[l]), row2(lambda_k1[l]), row2(lambda_q2[l]), row2(lambda_k2[l])]
    sub_g = row2(subln_g[l])
    ln_g, ln_b = row2(gmlp_ln_g[l]), row2(gmlp_ln_b[l])
    mix_g, ffn_g, fin_g = row2(norm_mix_g[l]), row2(norm_ffn_g[l]), row2(norm_final_g)
    wa, wb, wo = (w_branch_a[l].astype(BF16), w_branch_b[l].astype(BF16), w_out[l].astype(BF16))
    w1, w2 = w_ff1[l].astype(BF16), w_ff2[l].astype(BF16)

    xp = x_prompt.reshape(sp, d_model)
    q, k32, k16, v32, v16t, u, vg, gates = _token_path_in(
        xp, mix_g, w_in[l], ln_g, ln_b, d_attn, d_gmlp, _row_tile(sp, TM_INPROJ), BF16, "vt")
    att = _attn_prompt(q, k16, v16t, lams, subln_g[l].reshape(-1, 1), lam_init,
                       _row_tile(sp, ATTN_TQ))
    bsp_p = b_spatial[l].reshape(n_groups, CHUNK, 1)
    yp = _token_path_out(xp, att, u, vg, gates, w_spatial[l], bsp_p, CHUNK, wa, wb, wo, ffn_g,
                         w1, w2, fin_g, _row_tile(sp, TM_MIXER), _row_tile(sp, TM_FFN))
    new_k_prompt = k32.reshape(1, bp, sp, n_heads, V_DIM)
    new_v_prompt = v32.reshape(1, bp, sp, n_heads, V_DIM)

    ms = bs * ts
    xs = x_sample.reshape(ms, d_model)
    q, k32, k16, v32, v16, u, vg, gates = _token_path_in(
        xs, mix_g, w_in[l], ln_g, ln_b, d_attn, d_gmlp, _row_tile(ms, TM_INPROJ), F32, "kv")
    q5 = q.reshape(bs, ts, n_heads, 2, HEAD_DIM).transpose(0, 2, 3, 1, 4)
    eye = jnp.eye(2, dtype=BF16)
    q_rows = (q5[:, :, :, :, None, :] * eye[None, None, :, None, :, None]).reshape(
        bs, n_heads * 2 * ts, V_DIM)
    k_new = k16.reshape(bs, ts * n_heads, V_DIM)
    v_new = v16.reshape(bs, ts * n_heads, V_DIM)
    ck = cache_k[l].reshape(-1, page * n_heads, V_DIM)
    cv = cache_v[l].reshape(-1, page * n_heads, V_DIM)
    n_pages = page_table.shape[1]
    pages = next(p for p in (16, 8, 4, 2, 1) if n_pages % p == 0)
    att_s = _attn_decode(q_rows, k_new, v_new, ck, cv, page_table, lams, sub_g, lam_init, ts,
                         n_heads, pages)
    att_s = att_s.reshape(ms, d_attn)
    reps = CHUNK // ts
    wsp_s = jnp.tile(w_spatial[l][:, :ts, :ts], (1, reps, reps))
    bsp_s = jnp.tile(b_spatial[l][:, :ts], (1, reps)).reshape(n_groups, CHUNK, 1)
    ys = _token_path_out(xs, att_s, u, vg, gates, wsp_s, bsp_s, ts, wa, wb, wo, ffn_g,
                         w1, w2, fin_g, _row_tile(ms, TM_MIXER), _row_tile(ms, TM_FFN))

    return (yp.reshape(bp, sp, d_model), ys.reshape(bs, ts, d_model), new_k_prompt, new_v_prompt,
            k32.reshape(1, bs, ts, n_heads, V_DIM), v32.reshape(1, bs, ts, n_heads, V_DIM),
            vg.reshape(1, bs, ts, d_gmlp))
```

```python
import functools
import math

import jax
import jax.numpy as jnp
from jax import lax
from jax.experimental import pallas as pl
from jax.experimental.pallas import tpu as pltpu

F32 = jnp.float32
BF16 = jnp.bfloat16

HEAD_DIM = 64
V_DIM = 2 * HEAD_DIM
CHUNK = 128
NEG_INF = -1e30
LOG2_E = math.log2(math.e)
RMS_EPS = 1e-6
LN_EPS = 1e-5
MIB = 1024 * 1024
V7X_VMEM_BUDGET = 56 * MIB
TM_INPROJ = 1024
TM_MIXER = 256
TM_FFN = 512


def _row_tile(m, target):
    t = min(m, target)
    assert m % t == 0
    return t


def _params(sem, vmem=V7X_VMEM_BUDGET):
    return pltpu.CompilerParams(dimension_semantics=sem, vmem_limit_bytes=vmem)


def _rms(x, g):
    return x * lax.rsqrt(jnp.mean(x * x, axis=-1, keepdims=True) + RMS_EPS) * g


def _lambda(lq1, lk1, lq2, lk2, lam_init):
    a = jnp.exp(jnp.sum(lq1[...] * lk1[...], axis=1, keepdims=True))
    b = jnp.exp(jnp.sum(lq2[...] * lk2[...], axis=1, keepdims=True))
    return a - b + lam_init


def _rmsnorm_kernel(x_ref, g_ref, o_ref):
    o_ref[...] = _rms(x_ref[...], g_ref[...]).astype(o_ref.dtype)


def _rmsnorm(x, g, tm):
    m, d = x.shape
    return pl.pallas_call(
        _rmsnorm_kernel,
        out_shape=jax.ShapeDtypeStruct((m, d), BF16),
        grid=(m // tm,),
        in_specs=[pl.BlockSpec((tm, d), lambda i: (i, 0)),
                  pl.BlockSpec((1, d), lambda i: (0, 0))],
        out_specs=pl.BlockSpec((tm, d), lambda i: (i, 0)),
        compiler_params=_params(("arbitrary",)),
        name="rmsnorm",
    )(x, g)


def _inproj_kernel(h_ref, w_ref, *refs, kind, scale):
    wbf_ref = refs[-1]

    @pl.when(pl.program_id(1) == 0)
    def _cast_weights():
        wbf_ref[...] = w_ref[...].astype(BF16)

    z = jnp.dot(h_ref[...], wbf_ref[...], preferred_element_type=F32)
    if kind == "q":
        refs[0][...] = (z * scale).astype(BF16)
    elif kind == "kv":
        refs[0][...] = z
        refs[1][...] = z.astype(BF16)
    elif kind == "vt":
        refs[0][...] = z
        tk = refs[1].shape[2]
        for c in range(refs[1].shape[0]):
            refs[1][c] = z[c * tk:(c + 1) * tk].T.astype(BF16)
    elif kind == "u":
        refs[0][...] = jax.nn.gelu(z).astype(refs[0].dtype)
    elif kind == "vs":
        g_ref, b_ref, o_ref = refs[0], refs[1], refs[2]
        y = jax.nn.gelu(z)
        mu = jnp.mean(y, axis=-1, keepdims=True)
        yc = y - mu
        var = jnp.mean(yc * yc, axis=-1, keepdims=True)
        o_ref[...] = (yc * lax.rsqrt(var + LN_EPS) * g_ref[...] + b_ref[...]).astype(o_ref.dtype)
    elif kind == "gates":
        refs[0][...] = jax.nn.sigmoid(z).astype(refs[0].dtype)
    else:
        raise ValueError(kind)


def _inproj(h, w, col0, ncols, kind, out_dtypes, tm, extra=(), scale=1.0, tn=1024):
    m, d = h.shape
    assert col0 % tn == 0 and ncols % tn == 0
    j0 = col0 // tn
    in_specs = [pl.BlockSpec((tm, d), lambda j, i: (i, 0)),
                pl.BlockSpec((d, tn), lambda j, i: (0, j0 + j))]
    in_specs += [pl.BlockSpec((1, tn), lambda j, i: (0, 0)) for _ in extra]
    out_shape = [jax.ShapeDtypeStruct((m, ncols), dt) for dt in out_dtypes]
    out_specs = [pl.BlockSpec((tm, tn), lambda j, i: (i, j)) for _ in out_dtypes]
    if kind == "vt":
        tk = min(tm, ATTN_TK)
        out_shape[1] = jax.ShapeDtypeStruct((m // tk, ncols, tk), out_dtypes[1])
        out_specs[1] = pl.BlockSpec((tm // tk, tn, tk), lambda j, i: (i, j, 0))
    outs = pl.pallas_call(
        functools.partial(_inproj_kernel, kind=kind, scale=scale),
        out_shape=out_shape,
        grid=(ncols // tn, m // tm),
        in_specs=in_specs,
        out_specs=out_specs,
        scratch_shapes=[pltpu.VMEM((d, tn), BF16)],
        compiler_params=_params(("arbitrary", "arbitrary")),
        name="inproj_" + kind,
    )(h, w, *extra)
    return outs


ATTN_TK = 512
ATTN_TQ = 1024
L_ROWS = 16
LANE_GROUP = 512


def _attn_prompt_kernel(lq1, lk1, lq2, lk2, sg_ref, q_ref, k_ref, vt_ref, o_ref, m_ref, acc_ref,
                        s_ref, *, tq, tk, lam_init):
    qi = pl.program_id(1)
    q = q_ref[...]
    lane = lax.broadcasted_iota(jnp.int32, q.shape, 1)
    zero = jnp.zeros_like(q)
    qb = jnp.concatenate([jnp.where(lane < HEAD_DIM, q, zero),
                          jnp.where(lane >= HEAD_DIM, q, zero)], axis=0)
    ones = jnp.ones((L_ROWS, tk), BF16)
    nt = (((1,), (1,)), ((), ()))

    m_ref[...] = jnp.full(m_ref.shape, NEG_INF, F32)
    acc_ref[...] = jnp.zeros(acc_ref.shape, F32)

    n_groups = 2 * tq // LANE_GROUP
    lanes = [slice(g * LANE_GROUP, (g + 1) * LANE_GROUP) for g in range(n_groups)]
    n_full = qi * (tq // tk)

    def scores(ki, g):
        k = k_ref[pl.ds(pl.multiple_of(ki * tk, tk), tk), :]
        return lax.dot_general(k, qb[lanes[g]], nt, preferred_element_type=F32)

    def first_query(g):
        return (g * LANE_GROUP) % tq

    def causal(s, g, j):
        key = lax.broadcasted_iota(jnp.int32, s.shape, 0) + j * tk
        qry = lax.broadcasted_iota(jnp.int32, s.shape, 1) + first_query(g)
        return jnp.where(key <= qry, s, NEG_INF)

    def values(ki):
        return jnp.concatenate([vt_ref[ki], ones], axis=0)

    def softmax_pv(s, g, vt):
        ls = lanes[g]
        m_old = m_ref[:, ls]
        m_new = jnp.maximum(m_old, jnp.max(s, axis=0, keepdims=True))
        alpha = jnp.exp2(m_old - m_new)
        p = jnp.exp2(s - m_new).astype(BF16)
        acc_ref[:, ls] = alpha * acc_ref[:, ls] + jnp.dot(vt, p, preferred_element_type=F32)
        m_ref[:, ls] = m_new

    s_ref[...] = scores(0, 0)

    def body(ki, carry):
        vt = values(ki)
        s = s_ref[...]
        for g in range(n_groups):
            if g + 1 < n_groups:
                s_next = scores(ki, g + 1)
            else:
                s_ref[...] = scores(ki + 1, 0)
            softmax_pv(s, g, vt)
            s = s_next
        return carry

    lax.fori_loop(0, n_full, body, 0)

    for j in range(tq // tk):
        ki = n_full + j
        vt = values(ki)
        active = [g for g in range(n_groups) if first_query(g) + LANE_GROUP - 1 >= j * tk]

        def masked_scores(g):
            s = s_ref[...] if (j == 0 and g == 0) else scores(ki, g)
            fully_visible = first_query(g) >= (j + 1) * tk - 1
            return s if fully_visible else causal(s, g, j)

        s = masked_scores(active[0])
        for idx, g in enumerate(active):
            if idx + 1 < len(active):
                s_next = masked_scores(active[idx + 1])
            softmax_pv(s, g, vt)
            s = s_next

    acc = acc_ref[...]
    n = acc[:V_DIM] / acc[V_DIM:V_DIM + 1]
    lam = _lambda(lq1, lk1, lq2, lk2, lam_init)
    att = n[:, :tq] - lam * n[:, tq:]
    y = att * lax.rsqrt(jnp.mean(att * att, axis=0, keepdims=True) + RMS_EPS) * sg_ref[...]
    o_ref[...] = (y * (1.0 - lam_init)).T.astype(o_ref.dtype)


def _attn_prompt(q, k, vt, lams, subln_g_col, lam_init, tq):
    s, w = q.shape
    n_heads = w // V_DIM
    tk = vt.shape[2]
    assert vt.shape == (s // tk, w, tk) and tq % tk == 0 and s % tq == 0
    assert (2 * tq) % LANE_GROUP == 0 and 2 * tq >= 2 * LANE_GROUP and tq % LANE_GROUP == 0
    lam_specs = [pl.BlockSpec((1, HEAD_DIM), lambda h, i: (0, 0)) for _ in lams]
    return pl.pallas_call(
        functools.partial(_attn_prompt_kernel, tq=tq, tk=tk, lam_init=lam_init),
        out_shape=jax.ShapeDtypeStruct((s, w), BF16),
        grid=(n_heads, s // tq),
        in_specs=lam_specs + [
            pl.BlockSpec((V_DIM, 1), lambda h, i: (0, 0)),
            pl.BlockSpec((tq, V_DIM), lambda h, i: (i, h)),
            pl.BlockSpec((s, V_DIM), lambda h, i: (0, h)),
            pl.BlockSpec((s // tk, V_DIM, tk), lambda h, i: (0, h, 0)),
        ],
        out_specs=pl.BlockSpec((tq, V_DIM), lambda h, i: (i, h)),
        scratch_shapes=[pltpu.VMEM((1, 2 * tq), F32), pltpu.VMEM((V_DIM + L_ROWS, 2 * tq), F32),
                        pltpu.VMEM((tk, LANE_GROUP), F32)],
        compiler_params=_params(("arbitrary", "arbitrary")),
        name="attn_prompt",
    )(*lams, subln_g_col, q, k, vt)


DECODE_PAGES = 8
DECODE_SUB_PAGES = 2
DECODE_RING = 4

def _attn_decode_kernel(pt_ref, lq1, lk1, lq2, lk2, sg_ref, q_ref, kn_ref, vn_ref, ck_hbm, cv_hbm,
                        o_ref, m_ref, l_ref, acc_ref, kbuf, vbuf, sem,
                        *, pages, sub, t_new, n_heads, lam_init):
    n_buf = kbuf.shape[0]
    page_rows = kbuf.shape[1] // pages
    t = pl.program_id(1)
    steps_per_seq = pl.num_programs(1)
    g = pl.program_id(0) * steps_per_seq + t
    n_steps = pl.num_programs(0) * steps_per_seq

    def step_copies(step):
        slot = lax.rem(step, n_buf)
        out = []
        for i in range(pages):
            pg = pt_ref[step * pages + i]
            rows = pl.ds(i * page_rows, page_rows)
            out.append(pltpu.make_async_copy(ck_hbm.at[pg], kbuf.at[slot, rows], sem.at[0, slot]))
            out.append(pltpu.make_async_copy(cv_hbm.at[pg], vbuf.at[slot, rows], sem.at[1, slot]))
        return out

    @pl.when(g == 0)
    def _prime():
        for step in range(n_buf - 1):
            for c in step_copies(step):
                c.start()

    @pl.when(g + (n_buf - 1) < n_steps)
    def _prefetch():
        for c in step_copies(g + (n_buf - 1)):
            c.start()

    for c in step_copies(g):
        c.wait()
    slot = lax.rem(g, n_buf)

    q = q_ref[0]
    rows_per_head = 2 * t_new
    nt = (((1,), (1,)), ((), ()))

    def own_head(shape):
        row = lax.broadcasted_iota(jnp.int32, shape, 0)
        col = lax.broadcasted_iota(jnp.int32, shape, 1)
        return (col % n_heads) == (row // rows_per_head), row, col

    @pl.when(t == 0)
    def _init_from_new_tokens():
        s = lax.dot_general(q, kn_ref[0], nt, preferred_element_type=F32)
        same, row, col = own_head(s.shape)
        s = jnp.where(same & ((col // n_heads) <= (row % t_new)), s, NEG_INF)
        m = jnp.max(s, axis=1, keepdims=True)
        p = jnp.exp2(s - m)
        m_ref[...] = m
        l_ref[...] = jnp.sum(p, axis=1, keepdims=True)
        acc_ref[...] = jnp.dot(p.astype(BF16), vn_ref[0], preferred_element_type=F32)

    def page(buf, i):
        return buf[slot, pl.ds(i * page_rows, page_rows), :].astype(BF16)

    def scores(blk):
        s = jnp.concatenate(
            [lax.dot_general(q, page(kbuf, blk * sub + i), nt, preferred_element_type=F32)
             for i in range(sub)], axis=1)
        same, _, _ = own_head(s.shape)
        return jnp.where(same, s, NEG_INF)

    def softmax_pv(s, blk):
        m_old = m_ref[...]
        m_new = jnp.maximum(m_old, jnp.max(s, axis=1, keepdims=True))
        alpha = jnp.exp2(m_old - m_new)
        p = jnp.exp2(s - m_new)
        l_ref[...] = alpha * l_ref[...] + jnp.sum(p, axis=1, keepdims=True)
        pb = p.astype(BF16)
        pv = None
        for i in range(sub):
            d = jnp.dot(pb[:, i * page_rows:(i + 1) * page_rows], page(vbuf, blk * sub + i),
                        preferred_element_type=F32)
            pv = d if pv is None else pv + d
        acc_ref[...] = alpha * acc_ref[...] + pv
        m_ref[...] = m_new

    n_blocks = pages // sub
    s = scores(0)
    for blk in range(n_blocks):
        if blk + 1 < n_blocks:
            s_next = scores(blk + 1)
        softmax_pv(s, blk)
        s = s_next

    @pl.when(t == pl.num_programs(1) - 1)
    def _finalize():
        lam = _lambda(lq1, lk1, lq2, lk2, lam_init)
        n = acc_ref[...] / l_ref[...]
        for h in range(n_heads):
            r0 = h * rows_per_head
            att = n[r0:r0 + t_new] - lam * n[r0 + t_new:r0 + rows_per_head]
            o_ref[0, :, h * V_DIM:(h + 1) * V_DIM] = _rms(att, sg_ref[...]) * (1.0 - lam_init)


def _attn_decode(q, k_new, v_new, cache_k, cache_v, page_table, lams, subln_g, lam_init, t_new,
                 n_heads, pages):
    bs, r, _ = q.shape
    n_pages = page_table.shape[1]
    sub = math.gcd(pages, DECODE_SUB_PAGES)
    assert n_pages % pages == 0 and bs * (n_pages // pages) >= DECODE_RING - 1
    pt_flat = page_table.reshape(-1)
    page_rows = cache_k.shape[1]

    const = lambda b, t, pt: (0, 0)
    per_seq = lambda b, t, pt: (b, 0, 0)
    grid_spec = pltpu.PrefetchScalarGridSpec(
        num_scalar_prefetch=1,
        grid=(bs, n_pages // pages),
        in_specs=[pl.BlockSpec((1, HEAD_DIM), const) for _ in lams] + [
            pl.BlockSpec((1, V_DIM), const),
            pl.BlockSpec((1, r, V_DIM), per_seq),
            pl.BlockSpec((1,) + k_new.shape[1:], per_seq),
            pl.BlockSpec((1,) + v_new.shape[1:], per_seq),
            pl.BlockSpec(memory_space=pl.ANY),
            pl.BlockSpec(memory_space=pl.ANY),
        ],
        out_specs=pl.BlockSpec((1, t_new, n_heads * V_DIM), per_seq),
        scratch_shapes=[pltpu.VMEM((r, 1), F32), pltpu.VMEM((r, 1), F32), pltpu.VMEM((r, V_DIM), F32),
                        pltpu.VMEM((DECODE_RING, pages * page_rows, V_DIM), F32),
                        pltpu.VMEM((DECODE_RING, pages * page_rows, V_DIM), F32),
                        pltpu.SemaphoreType.DMA((2, DECODE_RING))],
    )
    return pl.pallas_call(
        functools.partial(_attn_decode_kernel, pages=pages, sub=sub, t_new=t_new, n_heads=n_heads,
                          lam_init=lam_init),
        out_shape=jax.ShapeDtypeStruct((bs, t_new, n_heads * V_DIM), F32),
        grid_spec=grid_spec,
        compiler_params=_params(("arbitrary", "arbitrary")),
        name="attn_decode",
    )(pt_flat, *lams, subln_g, q, k_new, v_new, cache_k, cache_v)


def _mixer_out_kernel(att_ref, u_ref, vg_ref, ga_ref, gb_ref, x_ref, wsp_ref, bsp_ref, wa_ref, wb_ref,
                      wo_ref, g_ref, y_ref, h_ref, ug_ref, *, chunk_len):
    tm = u_ref.shape[0]
    n_groups = wsp_ref.shape[0]
    row = lax.broadcasted_iota(jnp.int32, (CHUNK, CHUNK), 0)
    col = lax.broadcasted_iota(jnp.int32, (CHUNK, CHUNK), 1)
    mask = (row // chunk_len == col // chunk_len) & (col <= row)
    for g in range(n_groups):
        wg = jnp.where(mask, wsp_ref[g], 0.0).astype(BF16)
        bg = bsp_ref[g]
        cs = slice(g * CHUNK, (g + 1) * CHUNK)
        for r in range(tm // CHUNK):
            rs = slice(r * CHUNK, (r + 1) * CHUNK)
            sg = jnp.dot(wg, vg_ref[rs, cs].astype(BF16), preferred_element_type=F32) + bg
            ug_ref[rs, cs] = (u_ref[rs, cs].astype(F32) * sg).astype(BF16)
    a = jnp.dot(att_ref[...].astype(BF16), wa_ref[...], preferred_element_type=F32)
    gp = jnp.dot(ug_ref[...], wb_ref[...], preferred_element_type=F32)
    mm = (ga_ref[...].astype(F32) * a + gb_ref[...].astype(F32) * gp).astype(BF16)
    y = x_ref[...] + jnp.dot(mm, wo_ref[...], preferred_element_type=F32)
    y_ref[...] = y
    h_ref[...] = _rms(y, g_ref[...]).astype(h_ref.dtype)


def _mixer_out(att, u, vg, gates, x, wsp, bsp, wa, wb, wo, g, chunk_len, tm):
    m, d = x.shape
    d_in = att.shape[1]
    ng = wsp.shape[0]
    rows = lambda w: pl.BlockSpec((tm, w), lambda i: (i, 0))
    resident = lambda shape: pl.BlockSpec(shape, lambda i: (0,) * len(shape),
                                          pipeline_mode=pl.Buffered(1))
    return pl.pallas_call(
        functools.partial(_mixer_out_kernel, chunk_len=chunk_len),
        out_shape=[jax.ShapeDtypeStruct((m, d), F32), jax.ShapeDtypeStruct((m, d), BF16)],
        grid=(m // tm,),
        in_specs=[rows(d_in), rows(d_in), rows(d_in),
                  pl.BlockSpec((tm, d), lambda i: (i, 0)),
                  pl.BlockSpec((tm, d), lambda i: (i, 1)),
                  rows(d),
                  resident((ng, CHUNK, CHUNK)), resident((ng, CHUNK, 1)),
                  resident((d_in, d)), resident((d_in, d)), resident((d, d)), resident((1, d))],
        out_specs=[rows(d), rows(d)],
        scratch_shapes=[pltpu.VMEM((tm, d_in), BF16)],
        compiler_params=_params(("arbitrary",)),
        name="mixer_out",
    )(att, u, vg, gates, gates, x, wsp, bsp, wa, wb, wo, g)


def _ffn_kernel(h_ref, y_ref, w1_ref, w2_ref, g_ref, o_ref):
    f = pl.program_id(1)

    @pl.when(f == 0)
    def _init():
        o_ref[...] = y_ref[...]

    h = h_ref[...]
    half = w1_ref.shape[1] // 2
    a = [jnp.dot(h, w1_ref[:, c * half:(c + 1) * half], preferred_element_type=F32) for c in range(2)]
    part = None
    for c in range(2):
        r = jnp.square(jnp.maximum(a[c], 0.0)).astype(BF16)
        d = jnp.dot(r, w2_ref[c * half:(c + 1) * half, :], preferred_element_type=F32)
        part = d if part is None else part + d
    o_ref[...] += part

    @pl.when(f == pl.num_programs(1) - 1)
    def _final_norm():
        o_ref[...] = _rms(o_ref[...], g_ref[...])


def _ffn(h, y, w1, w2, g, tm, tf=1024):
    m, d = h.shape
    d_ff = w1.shape[1]
    return pl.pallas_call(
        _ffn_kernel,
        out_shape=jax.ShapeDtypeStruct((m, d), F32),
        grid=(m // tm, d_ff // tf),
        in_specs=[pl.BlockSpec((tm, d), lambda i, f: (i, 0)),
                  pl.BlockSpec((tm, d), lambda i, f: (i, 0)),
                  pl.BlockSpec((d, tf), lambda i, f: (0, f)),
                  pl.BlockSpec((tf, d), lambda i, f: (f, 0)),
                  pl.BlockSpec((1, d), lambda i, f: (0, 0))],
        out_specs=pl.BlockSpec((tm, d), lambda i, f: (i, 0)),
        compiler_params=_params(("arbitrary", "arbitrary")),
        name="ffn",
    )(h, y, w1, w2, g)


def _token_path_in(x, norm_g, w_in, ln_g, ln_b, d_attn, d_gmlp, tm, vg_dtype, v_kind):
    d_model = x.shape[1]
    h = _rmsnorm(x, norm_g, tm)
    c = 0
    (q,) = _inproj(h, w_in, c, d_attn, "q", [BF16], tm, scale=HEAD_DIM ** -0.5 * LOG2_E)
    c += d_attn
    k32, k16 = _inproj(h, w_in, c, d_attn, "kv", [F32, BF16], tm)
    c += d_attn
    v32, v16 = _inproj(h, w_in, c, d_attn, v_kind, [F32, BF16], tm)
    c += d_attn
    (u,) = _inproj(h, w_in, c, d_gmlp, "u", [BF16], tm)
    c += d_gmlp
    (vg,) = _inproj(h, w_in, c, d_gmlp, "vs", [vg_dtype], tm, extra=(ln_g, ln_b))
    c += d_gmlp
    (gates,) = _inproj(h, w_in, c, 2 * d_model, "gates", [BF16], tm)
    return q, k32, k16, v32, v16, u, vg, gates


def _token_path_out(x, att, u, vg, gates, wsp, bsp, chunk_len, wa, wb, wo, norm_ffn_g, w1, w2,
                    norm_final_g, tm_mix, tm_ffn):
    y, h2 = _mixer_out(att, u, vg, gates, x, wsp, bsp, wa, wb, wo, norm_ffn_g, chunk_len, tm_mix)
    return _ffn(h2, y, w1, w2, norm_final_g, tm_ffn)


def kernel(x_prompt, x_sample, cache_k, cache_v, page_table, norm_mix_g, w_in, lambda_q1, lambda_k1,
           lambda_q2, lambda_k2, subln_g, gmlp_ln_g, gmlp_ln_b, w_spatial, b_spatial, w_branch_a,
           w_branch_b, w_out, norm_ffn_g, w_ff1, w_ff2, norm_final_g):
    bp, sp, d_model = x_prompt.shape
    bs, ts, _ = x_sample.shape
    depth = w_in.shape[0]
    assert bp == 1 and depth == 1, "the final norm is fused into the MLP kernel of the single layer"
    n_heads = cache_k.shape[3]
    d_attn = n_heads * V_DIM
    d_gmlp = gmlp_ln_g.shape[1]
    n_groups = w_spatial.shape[1]
    page = cache_k.shape[2]
    assert sp % CHUNK == 0 and CHUNK % ts == 0 and (bs * ts) % CHUNK == 0

    l = 0
    lam_init = 0.8 - 0.6 * math.exp(-0.3 * l)
    row2 = lambda a: a.reshape(1, -1)
    lams = [row2(lambda_q1[l]), row2(lambda_k1[l]), row2(lambda_q2[l]), row2(lambda_k2[l])]
    sub_g = row2(subln_g[l])
    ln_g, ln_b = row2(gmlp_ln_g[l]), row2(gmlp_ln_b[l])
    mix_g, ffn_g, fin_g = row2(norm_mix_g[l]), row2(norm_ffn_g[l]), row2(norm_final_g)
    wa, wb, wo = (w_branch_a[l].astype(BF16), w_branch_b[l].astype(BF16), w_out[l].astype(BF16))
    w1, w2 = w_ff1[l].astype(BF16), w_ff2[l].astype(BF16)

    xp = x_prompt.reshape(sp, d_model)
    q, k32, k16, v32, v16t, u, vg, gates = _token_path_in(
        xp, mix_g, w_in[l], ln_g, ln_b, d_attn, d_gmlp, _row_tile(sp, TM_INPROJ), BF16, "vt")
    att = _attn_prompt(q, k16, v16t, lams, subln_g[l].reshape(-1, 1), lam_init,
                       _row_tile(sp, ATTN_TQ))
    bsp_p = b_spatial[l].reshape(n_groups, CHUNK, 1)
    yp = _token_path_out(xp, att, u, vg, gates, w_spatial[l], bsp_p, CHUNK, wa, wb, wo, ffn_g,
                         w1, w2, fin_g, _row_tile(sp, TM_MIXER), _row_tile(sp, TM_FFN))
    new_k_prompt = k32.reshape(1, bp, sp, n_heads, V_DIM)
    new_v_prompt = v32.reshape(1, bp, sp, n_heads, V_DIM)

    ms = bs * ts
    xs = x_sample.reshape(ms, d_model)
    q, k32, k16, v32, v16, u, vg, gates = _token_path_in(
        xs, mix_g, w_in[l], ln_g, ln_b, d_attn, d_gmlp, _row_tile(ms, TM_INPROJ), F32, "kv")
    q5 = q.reshape(bs, ts, n_heads, 2, HEAD_DIM).transpose(0, 2, 3, 1, 4)
    eye = jnp.eye(2, dtype=BF16)
    q_rows = (q5[:, :, :, :, None, :] * eye[None, None, :, None, :, None]).reshape(
        bs, n_heads * 2 * ts, V_DIM)
    k_new = k16.reshape(bs, ts * n_heads, V_DIM)
    v_new = v16.reshape(bs, ts * n_heads, V_DIM)
    ck = cache_k[l].reshape(-1, page * n_heads, V_DIM)
    cv = cache_v[l].reshape(-1, page * n_heads, V_DIM)
    n_pages = page_table.shape[1]
    pages = math.gcd(n_pages, DECODE_PAGES)
    att_s = _attn_decode(q_rows, k_new, v_new, ck, cv, page_table, lams, sub_g, lam_init, ts,
                         n_heads, pages)
    att_s = att_s.reshape(ms, d_attn)
    reps = CHUNK // ts
    wsp_s = jnp.tile(w_spatial[l][:, :ts, :ts], (1, reps, reps))
    bsp_s = jnp.tile(b_spatial[l][:, :ts], (1, reps)).reshape(n_groups, CHUNK, 1)
    ys = _token_path_out(xs, att_s, u, vg, gates, wsp_s, bsp_s, ts, wa, wb, wo, ffn_g,
                         w1, w2, fin_g, _row_tile(ms, TM_MIXER), _row_tile(ms, TM_FFN))

    return (yp.reshape(bp, sp, d_model), ys.reshape(bs, ts, d_model), new_k_prompt, new_v_prompt,
            k32.reshape(1, bs, ts, n_heads, V_DIM), v32.reshape(1, bs, ts, n_heads, V_DIM),
            vg.reshape(1, bs, ts, d_gmlp))
```

```python
import functools
import math

import jax
import jax.numpy as jnp
from jax import lax
from jax.experimental import pallas as pl
from jax.experimental.pallas import tpu as pltpu

F32 = jnp.float32
BF16 = jnp.bfloat16

HEAD_DIM = 64
V_DIM = 2 * HEAD_DIM
CHUNK = 128
NEG_INF = -1e30
LOG2_E = math.log2(math.e)
RMS_EPS = 1e-6
LN_EPS = 1e-5
MIB = 1024 * 1024
V7X_VMEM_BUDGET = 56 * MIB
TM_QNORM = 512
TM_INPROJ = 1024
TM_MIXER = 256
TM_FFN = 512


def _row_tile(m, target):
    t = min(m, target)
    assert m % t == 0
    return t


def _params(sem, vmem=V7X_VMEM_BUDGET):
    return pltpu.CompilerParams(dimension_semantics=sem, vmem_limit_bytes=vmem)


def _rms(x, g):
    return x * lax.rsqrt(jnp.mean(x * x, axis=-1, keepdims=True) + RMS_EPS) * g


def _lambda(lq1, lk1, lq2, lk2, lam_init):
    a = jnp.exp(jnp.sum(lq1[...] * lk1[...], axis=1, keepdims=True))
    b = jnp.exp(jnp.sum(lq2[...] * lk2[...], axis=1, keepdims=True))
    return a - b + lam_init


def _inproj_kernel(h_ref, w_ref, *refs, kind, scale):
    wbf_ref = refs[-1]

    @pl.when(pl.program_id(1) == 0)
    def _cast_weights():
        wbf_ref[...] = w_ref[...].astype(BF16)

    if kind == "q":
        h = _rms(h_ref[...], refs[0][...]).astype(BF16)
        refs[2][...] = h
        z = jnp.dot(h, wbf_ref[...], preferred_element_type=F32)
        refs[1][...] = (z * scale).astype(BF16)
        return
    z = jnp.dot(h_ref[...], wbf_ref[...], preferred_element_type=F32)
    if kind == "kv":
        refs[0][...] = z
        refs[1][...] = z.astype(BF16)
    elif kind == "vt":
        refs[0][...] = z
        tk = refs[1].shape[2]
        for c in range(refs[1].shape[0]):
            refs[1][c] = z[c * tk:(c + 1) * tk].T.astype(BF16)
    elif kind == "u":
        refs[0][...] = jax.nn.gelu(z).astype(refs[0].dtype)
    elif kind == "vs":
        g_ref, b_ref, o_ref = refs[0], refs[1], refs[2]
        y = jax.nn.gelu(z)
        mu = jnp.mean(y, axis=-1, keepdims=True)
        yc = y - mu
        var = jnp.mean(yc * yc, axis=-1, keepdims=True)
        o_ref[...] = (yc * lax.rsqrt(var + LN_EPS) * g_ref[...] + b_ref[...]).astype(o_ref.dtype)
    elif kind == "gates":
        refs[0][...] = jax.nn.sigmoid(z).astype(refs[0].dtype)
    else:
        raise ValueError(kind)


def _inproj(h, w, col0, ncols, kind, out_dtypes, tm, extra=(), scale=1.0, tn=1024):
    m, d = h.shape
    assert col0 % tn == 0 and ncols % tn == 0
    j0 = col0 // tn
    in_specs = [pl.BlockSpec((tm, d), lambda j, i: (i, 0)),
                pl.BlockSpec((d, tn), lambda j, i: (0, j0 + j))]
    in_specs += [pl.BlockSpec((1, e.shape[1]), lambda j, i: (0, 0)) for e in extra]
    out_shape = [jax.ShapeDtypeStruct((m, ncols), dt) for dt in out_dtypes]
    out_specs = [pl.BlockSpec((tm, tn), lambda j, i: (i, j)) for _ in out_dtypes]
    if kind == "q":
        assert ncols == tn
        out_shape[1] = jax.ShapeDtypeStruct((m, d), out_dtypes[1])
        out_specs[1] = pl.BlockSpec((tm, d), lambda j, i: (i, 0))
    if kind == "vt":
        tk = min(tm, ATTN_TK)
        out_shape[1] = jax.ShapeDtypeStruct((m // tk, ncols, tk), out_dtypes[1])
        out_specs[1] = pl.BlockSpec((tm // tk, tn, tk), lambda j, i: (i, j, 0))
    outs = pl.pallas_call(
        functools.partial(_inproj_kernel, kind=kind, scale=scale),
        out_shape=out_shape,
        grid=(ncols // tn, m // tm),
        in_specs=in_specs,
        out_specs=out_specs,
        scratch_shapes=[pltpu.VMEM((d, tn), BF16)],
        compiler_params=_params(("arbitrary", "arbitrary")),
        name="inproj_" + kind,
    )(h, w, *extra)
    return outs


ATTN_TK = 512
ATTN_TQ = 1024
L_ROWS = 16
LANE_GROUP = 512


def _attn_prompt_kernel(lq1, lk1, lq2, lk2, sg_ref, q_ref, k_ref, vt_ref, o_ref, m_ref, acc_ref,
                        s_ref, *, tq, tk, lam_init):
    qi = pl.program_id(1)
    q = q_ref[...]
    lane = lax.broadcasted_iota(jnp.int32, q.shape, 1)
    zero = jnp.zeros_like(q)
    qb = jnp.concatenate([jnp.where(lane < HEAD_DIM, q, zero),
                          jnp.where(lane >= HEAD_DIM, q, zero)], axis=0)
    qbt = qb.astype(F32).T.astype(BF16)
    ones = jnp.ones((L_ROWS, tk), BF16)

    m_ref[...] = jnp.full(m_ref.shape, NEG_INF, F32)
    acc_ref[...] = jnp.zeros(acc_ref.shape, F32)

    n_groups = 2 * tq // LANE_GROUP
    lanes = [slice(g * LANE_GROUP, (g + 1) * LANE_GROUP) for g in range(n_groups)]
    n_full = qi * (tq // tk)

    def scores(ki, g):
        k = k_ref[pl.ds(pl.multiple_of(ki * tk, tk), tk), :]
        return jnp.dot(k, qbt[:, lanes[g]], preferred_element_type=F32)

    def first_query(g):
        return (g * LANE_GROUP) % tq

    def causal(s, g, j):
        key = lax.broadcasted_iota(jnp.int32, s.shape, 0) + j * tk
        qry = lax.broadcasted_iota(jnp.int32, s.shape, 1) + first_query(g)
        return jnp.where(key <= qry, s, NEG_INF)

    def values(ki):
        return jnp.concatenate([vt_ref[ki], ones], axis=0)

    def softmax_pv(s, g, vt):
        ls = lanes[g]
        m_old = m_ref[:, ls]
        m_new = jnp.maximum(m_old, jnp.max(s, axis=0, keepdims=True))
        alpha = jnp.exp2(m_old - m_new)
        p = jnp.exp2(s - m_new).astype(BF16)
        acc_ref[:, ls] = alpha * acc_ref[:, ls] + jnp.dot(vt, p, preferred_element_type=F32)
        m_ref[:, ls] = m_new

    s_ref[...] = scores(0, 0)

    def body(ki, carry):
        vt = values(ki)
        s = s_ref[...]
        for g in range(n_groups):
            if g + 1 < n_groups:
                s_next = scores(ki, g + 1)
            else:
                s_ref[...] = scores(ki + 1, 0)
            softmax_pv(s, g, vt)
            s = s_next
        return carry

    lax.fori_loop(0, n_full, body, 0)

    pairs = [(j, g) for j in range(tq // tk) for g in range(n_groups)
             if first_query(g) + LANE_GROUP - 1 >= j * tk]
    vts = [values(n_full + j) for j in range(tq // tk)]

    def masked_scores(j, g):
        s = s_ref[...] if (j == 0 and g == 0) else scores(n_full + j, g)
        fully_visible = first_query(g) >= (j + 1) * tk - 1
        return s if fully_visible else causal(s, g, j)

    s = masked_scores(*pairs[0])
    for idx, (j, g) in enumerate(pairs):
        if idx + 1 < len(pairs):
            s_next = masked_scores(*pairs[idx + 1])
        softmax_pv(s, g, vts[j])
        s = s_next

    acc = acc_ref[...]
    n = acc[:V_DIM] / acc[V_DIM:V_DIM + 1]
    lam = _lambda(lq1, lk1, lq2, lk2, lam_init)
    att = n[:, :tq] - lam * n[:, tq:]
    y = att * lax.rsqrt(jnp.mean(att * att, axis=0, keepdims=True) + RMS_EPS) * sg_ref[...]
    o_ref[...] = (y * (1.0 - lam_init)).T.astype(o_ref.dtype)


def _attn_prompt(q, k, vt, lams, subln_g_col, lam_init, tq):
    s, w = q.shape
    n_heads = w // V_DIM
    tk = vt.shape[2]
    assert vt.shape == (s // tk, w, tk) and tq % tk == 0 and s % tq == 0
    assert (2 * tq) % LANE_GROUP == 0 and 2 * tq >= 2 * LANE_GROUP and tq % LANE_GROUP == 0
    lam_specs = [pl.BlockSpec((1, HEAD_DIM), lambda h, i: (0, 0)) for _ in lams]
    return pl.pallas_call(
        functools.partial(_attn_prompt_kernel, tq=tq, tk=tk, lam_init=lam_init),
        out_shape=jax.ShapeDtypeStruct((s, w), BF16),
        grid=(n_heads, s // tq),
        in_specs=lam_specs + [
            pl.BlockSpec((V_DIM, 1), lambda h, i: (0, 0)),
            pl.BlockSpec((tq, V_DIM), lambda h, i: (i, h)),
            pl.BlockSpec((s, V_DIM), lambda h, i: (0, h)),
            pl.BlockSpec((s // tk, V_DIM, tk), lambda h, i: (0, h, 0)),
        ],
        out_specs=pl.BlockSpec((tq, V_DIM), lambda h, i: (i, h)),
        scratch_shapes=[pltpu.VMEM((1, 2 * tq), F32), pltpu.VMEM((V_DIM + L_ROWS, 2 * tq), F32),
                        pltpu.VMEM((tk, LANE_GROUP), F32)],
        compiler_params=_params(("arbitrary", "arbitrary")),
        name="attn_prompt",
    )(*lams, subln_g_col, q, k, vt)


DECODE_PAGES = 8
DECODE_SUB_PAGES = 2
DECODE_RING = 4

def _attn_decode_kernel(pt_ref, lq1, lk1, lq2, lk2, sg_ref, q_ref, kn_ref, vn_ref, ck_hbm, cv_hbm,
                        o_ref, m_ref, l_ref, acc_ref, kbuf, vbuf, sem,
                        *, pages, sub, t_new, n_heads, lam_init):
    n_buf = kbuf.shape[0]
    page_rows = kbuf.shape[1] // pages
    t = pl.program_id(1)
    steps_per_seq = pl.num_programs(1)
    g = pl.program_id(0) * steps_per_seq + t
    n_steps = pl.num_programs(0) * steps_per_seq

    def step_copies(step):
        slot = lax.rem(step, n_buf)
        out = []
        for i in range(pages):
            pg = pt_ref[step * pages + i]
            rows = pl.ds(i * page_rows, page_rows)
            out.append(pltpu.make_async_copy(ck_hbm.at[pg], kbuf.at[slot, rows], sem.at[0, slot]))
            out.append(pltpu.make_async_copy(cv_hbm.at[pg], vbuf.at[slot, rows], sem.at[1, slot]))
        return out

    @pl.when(g == 0)
    def _prime():
        for step in range(n_buf - 1):
            for c in step_copies(step):
                c.start()

    @pl.when(g + (n_buf - 1) < n_steps)
    def _prefetch():
        for c in step_copies(g + (n_buf - 1)):
            c.start()

    for c in step_copies(g):
        c.wait()
    slot = lax.rem(g, n_buf)

    q = q_ref[0]
    rows_per_head = 2 * t_new
    nt = (((1,), (1,)), ((), ()))

    def own_head(shape):
        row = lax.broadcasted_iota(jnp.int32, shape, 0)
        col = lax.broadcasted_iota(jnp.int32, shape, 1)
        return (col % n_heads) == (row // rows_per_head), row, col

    @pl.when(t == 0)
    def _init_from_new_tokens():
        s = lax.dot_general(q, kn_ref[0], nt, preferred_element_type=F32)
        same, row, col = own_head(s.shape)
        s = jnp.where(same & ((col // n_heads) <= (row % t_new)), s, NEG_INF)
        m = jnp.max(s, axis=1, keepdims=True)
        p = jnp.exp2(s - m)
        m_ref[...] = m
        l_ref[...] = jnp.sum(p, axis=1, keepdims=True)
        acc_ref[...] = jnp.dot(p.astype(BF16), vn_ref[0], preferred_element_type=F32)

    def page(buf, i):
        return buf[slot, pl.ds(i * page_rows, page_rows), :].astype(BF16)

    def scores(blk):
        s = jnp.concatenate(
            [lax.dot_general(q, page(kbuf, blk * sub + i), nt, preferred_element_type=F32)
             for i in range(sub)], axis=1)
        same, _, _ = own_head(s.shape)
        return jnp.where(same, s, NEG_INF)

    def softmax_pv(s, blk):
        m_old = m_ref[...]
        m_new = jnp.maximum(m_old, jnp.max(s, axis=1, keepdims=True))
        alpha = jnp.exp2(m_old - m_new)
        p = jnp.exp2(s - m_new)
        l_ref[...] = alpha * l_ref[...] + jnp.sum(p, axis=1, keepdims=True)
        pb = p.astype(BF16)
        pv = None
        for i in range(sub):
            d = jnp.dot(pb[:, i * page_rows:(i + 1) * page_rows], page(vbuf, blk * sub + i),
                        preferred_element_type=F32)
            pv = d if pv is None else pv + d
        acc_ref[...] = alpha * acc_ref[...] + pv
        m_ref[...] = m_new

    n_blocks = pages // sub
    s = scores(0)
    for blk in range(n_blocks):
        if blk + 1 < n_blocks:
            s_next = scores(blk + 1)
        softmax_pv(s, blk)
        s = s_next

    @pl.when(t == pl.num_programs(1) - 1)
    def _finalize():
        lam = _lambda(lq1, lk1, lq2, lk2, lam_init)
        n = acc_ref[...] / l_ref[...]
        for h in range(n_heads):
            r0 = h * rows_per_head
            att = n[r0:r0 + t_new] - lam * n[r0 + t_new:r0 + rows_per_head]
            o_ref[0, :, h * V_DIM:(h + 1) * V_DIM] = _rms(att, sg_ref[...]) * (1.0 - lam_init)


def _attn_decode(q, k_new, v_new, cache_k, cache_v, page_table, lams, subln_g, lam_init, t_new,
                 n_heads, pages):
    bs, r, _ = q.shape
    n_pages = page_table.shape[1]
    sub = math.gcd(pages, DECODE_SUB_PAGES)
    assert n_pages % pages == 0 and bs * (n_pages // pages) >= DECODE_RING - 1
    pt_flat = page_table.reshape(-1)
    page_rows = cache_k.shape[1]

    const = lambda b, t, pt: (0, 0)
    per_seq = lambda b, t, pt: (b, 0, 0)
    grid_spec = pltpu.PrefetchScalarGridSpec(
        num_scalar_prefetch=1,
        grid=(bs, n_pages // pages),
        in_specs=[pl.BlockSpec((1, HEAD_DIM), const) for _ in lams] + [
            pl.BlockSpec((1, V_DIM), const),
            pl.BlockSpec((1, r, V_DIM), per_seq),
            pl.BlockSpec((1,) + k_new.shape[1:], per_seq),
            pl.BlockSpec((1,) + v_new.shape[1:], per_seq),
            pl.BlockSpec(memory_space=pl.ANY),
            pl.BlockSpec(memory_space=pl.ANY),
        ],
        out_specs=pl.BlockSpec((1, t_new, n_heads * V_DIM), per_seq),
        scratch_shapes=[pltpu.VMEM((r, 1), F32), pltpu.VMEM((r, 1), F32), pltpu.VMEM((r, V_DIM), F32),
                        pltpu.VMEM((DECODE_RING, pages * page_rows, V_DIM), F32),
                        pltpu.VMEM((DECODE_RING, pages * page_rows, V_DIM), F32),
                        pltpu.SemaphoreType.DMA((2, DECODE_RING))],
    )
    return pl.pallas_call(
        functools.partial(_attn_decode_kernel, pages=pages, sub=sub, t_new=t_new, n_heads=n_heads,
                          lam_init=lam_init),
        out_shape=jax.ShapeDtypeStruct((bs, t_new, n_heads * V_DIM), F32),
        grid_spec=grid_spec,
        compiler_params=_params(("arbitrary", "arbitrary")),
        name="attn_decode",
    )(pt_flat, *lams, subln_g, q, k_new, v_new, cache_k, cache_v)


def _mixer_out_kernel(att_ref, u_ref, vg_ref, ga_ref, gb_ref, x_ref, wsp_ref, bsp_ref, wa_ref, wb_ref,
                      wo_ref, g_ref, y_ref, h_ref, ug_ref, *, chunk_len):
    tm = u_ref.shape[0]
    n_groups = wsp_ref.shape[0]
    row = lax.broadcasted_iota(jnp.int32, (CHUNK, CHUNK), 0)
    col = lax.broadcasted_iota(jnp.int32, (CHUNK, CHUNK), 1)
    mask = (row // chunk_len == col // chunk_len) & (col <= row)
    for g in range(n_groups):
        wg = jnp.where(mask, wsp_ref[g], 0.0).astype(BF16)
        bg = bsp_ref[g]
        cs = slice(g * CHUNK, (g + 1) * CHUNK)
        for r in range(tm // CHUNK):
            rs = slice(r * CHUNK, (r + 1) * CHUNK)
            sg = jnp.dot(wg, vg_ref[rs, cs].astype(BF16), preferred_element_type=F32) + bg
            ug_ref[rs, cs] = (u_ref[rs, cs].astype(F32) * sg).astype(BF16)
    a = jnp.dot(att_ref[...].astype(BF16), wa_ref[...], preferred_element_type=F32)
    gp = jnp.dot(ug_ref[...], wb_ref[...], preferred_element_type=F32)
    mm = (ga_ref[...].astype(F32) * a + gb_ref[...].astype(F32) * gp).astype(BF16)
    y = x_ref[...] + jnp.dot(mm, wo_ref[...], preferred_element_type=F32)
    y_ref[...] = y
    h_ref[...] = _rms(y, g_ref[...]).astype(h_ref.dtype)


def _mixer_out(att, u, vg, gates, x, wsp, bsp, wa, wb, wo, g, chunk_len, tm):
    m, d = x.shape
    d_in = att.shape[1]
    ng = wsp.shape[0]
    rows = lambda w: pl.BlockSpec((tm, w), lambda i: (i, 0))
    resident = lambda shape: pl.BlockSpec(shape, lambda i: (0,) * len(shape),
                                          pipeline_mode=pl.Buffered(1))
    return pl.pallas_call(
        functools.partial(_mixer_out_kernel, chunk_len=chunk_len),
        out_shape=[jax.ShapeDtypeStruct((m, d), F32), jax.ShapeDtypeStruct((m, d), BF16)],
        grid=(m // tm,),
        in_specs=[rows(d_in), rows(d_in), rows(d_in),
                  pl.BlockSpec((tm, d), lambda i: (i, 0)),
                  pl.BlockSpec((tm, d), lambda i: (i, 1)),
                  rows(d),
                  resident((ng, CHUNK, CHUNK)), resident((ng, CHUNK, 1)),
                  resident((d_in, d)), resident((d_in, d)), resident((d, d)), resident((1, d))],
        out_specs=[rows(d), rows(d)],
        scratch_shapes=[pltpu.VMEM((tm, d_in), BF16)],
        compiler_params=_params(("arbitrary",)),
        name="mixer_out",
    )(att, u, vg, gates, gates, x, wsp, bsp, wa, wb, wo, g)


def _ffn_kernel(h_ref, y_ref, w1_ref, w2_ref, g_ref, o_ref):
    f = pl.program_id(1)

    @pl.when(f == 0)
    def _init():
        o_ref[...] = y_ref[...]

    h = h_ref[...]
    half = w1_ref.shape[1] // 2
    a = [jnp.dot(h, w1_ref[:, c * half:(c + 1) * half], preferred_element_type=F32) for c in range(2)]
    part = None
    for c in range(2):
        r = jnp.square(jnp.maximum(a[c], 0.0)).astype(BF16)
        d = jnp.dot(r, w2_ref[c * half:(c + 1) * half, :], preferred_element_type=F32)
        part = d if part is None else part + d
    o_ref[...] += part

    @pl.when(f == pl.num_programs(1) - 1)
    def _final_norm():
        o_ref[...] = _rms(o_ref[...], g_ref[...])


def _ffn(h, y, w1, w2, g, tm, tf=1024):
    m, d = h.shape
    d_ff = w1.shape[1]
    return pl.pallas_call(
        _ffn_kernel,
        out_shape=jax.ShapeDtypeStruct((m, d), F32),
        grid=(m // tm, d_ff // tf),
        in_specs=[pl.BlockSpec((tm, d), lambda i, f: (i, 0)),
                  pl.BlockSpec((tm, d), lambda i, f: (i, 0)),
                  pl.BlockSpec((d, tf), lambda i, f: (0, f)),
                  pl.BlockSpec((tf, d), lambda i, f: (f, 0)),
                  pl.BlockSpec((1, d), lambda i, f: (0, 0))],
        out_specs=pl.BlockSpec((tm, d), lambda i, f: (i, 0)),
        compiler_params=_params(("arbitrary", "arbitrary")),
        name="ffn",
    )(h, y, w1, w2, g)


def _token_path_in(x, norm_g, w_in, ln_g, ln_b, d_attn, d_gmlp, tm, vg_dtype, v_kind):
    m, d_model = x.shape
    c = 0
    q, h = _inproj(x, w_in, c, d_attn, "q", [BF16, BF16], _row_tile(m, TM_QNORM), extra=(norm_g,),
                   scale=HEAD_DIM ** -0.5 * LOG2_E)
    c += d_attn
    k32, k16 = _inproj(h, w_in, c, d_attn, "kv", [F32, BF16], tm)
    c += d_attn
    v32, v16 = _inproj(h, w_in, c, d_attn, v_kind, [F32, BF16], tm)
    c += d_attn
    (u,) = _inproj(h, w_in, c, d_gmlp, "u", [BF16], tm)
    c += d_gmlp
    (vg,) = _inproj(h, w_in, c, d_gmlp, "vs", [vg_dtype], tm, extra=(ln_g, ln_b))
    c += d_gmlp
    (gates,) = _inproj(h, w_in, c, 2 * d_model, "gates", [BF16], tm)
    return q, k32, k16, v32, v16, u, vg, gates


def _token_path_out(x, att, u, vg, gates, wsp, bsp, chunk_len, wa, wb, wo, norm_ffn_g, w1, w2,
                    norm_final_g, tm_mix, tm_ffn):
    y, h2 = _mixer_out(att, u, vg, gates, x, wsp, bsp, wa, wb, wo, norm_ffn_g, chunk_len, tm_mix)
    return _ffn(h2, y, w1, w2, norm_final_g, tm_ffn)


def kernel(x_prompt, x_sample, cache_k, cache_v, page_table, norm_mix_g, w_in, lambda_q1, lambda_k1,
           lambda_q2, lambda_k2, subln_g, gmlp_ln_g, gmlp_ln_b, w_spatial, b_spatial, w_branch_a,
           w_branch_b, w_out, norm_ffn_g, w_ff1, w_ff2, norm_final_g):
    bp, sp, d_model = x_prompt.shape
    bs, ts, _ = x_sample.shape
    depth = w_in.shape[0]
    assert bp == 1 and depth == 1, "the final norm is fused into the MLP kernel of the single layer"
    n_heads = cache_k.shape[3]
    d_attn = n_heads * V_DIM
    d_gmlp = gmlp_ln_g.shape[1]
    n_groups = w_spatial.shape[1]
    page = cache_k.shape[2]
    assert sp % CHUNK == 0 and CHUNK % ts == 0 and (bs * ts) % CHUNK == 0

    l = 0
    lam_init = 0.8 - 0.6 * math.exp(-0.3 * l)
    row2 = lambda a: a.reshape(1, -1)
    lams = [row2(lambda_q1[l]), row2(lambda_k1[l]), row2(lambda_q2[l]), row2(lambda_k2[l])]
    sub_g = row2(subln_g[l])
    ln_g, ln_b = row2(gmlp_ln_g[l]), row2(gmlp_ln_b[l])
    mix_g, ffn_g, fin_g = row2(norm_mix_g[l]), row2(norm_ffn_g[l]), row2(norm_final_g)
    wa, wb, wo = (w_branch_a[l].astype(BF16), w_branch_b[l].astype(BF16), w_out[l].astype(BF16))
    w1, w2 = w_ff1[l].astype(BF16), w_ff2[l].astype(BF16)

    xp = x_prompt.reshape(sp, d_model)
    q, k32, k16, v32, v16t, u, vg, gates = _token_path_in(
        xp, mix_g, w_in[l], ln_g, ln_b, d_attn, d_gmlp, _row_tile(sp, TM_INPROJ), BF16, "vt")
    att = _attn_prompt(q, k16, v16t, lams, subln_g[l].reshape(-1, 1), lam_init,
                       _row_tile(sp, ATTN_TQ))
    bsp_p = b_spatial[l].reshape(n_groups, CHUNK, 1)
    yp = _token_path_out(xp, att, u, vg, gates, w_spatial[l], bsp_p, CHUNK, wa, wb, wo, ffn_g,
                         w1, w2, fin_g, _row_tile(sp, TM_MIXER), _row_tile(sp, TM_FFN))
    new_k_prompt = k32.reshape(1, bp, sp, n_heads, V_DIM)
    new_v_prompt = v32.reshape(1, bp, sp, n_heads, V_DIM)

    ms = bs * ts
    xs = x_sample.reshape(ms, d_model)
    q, k32, k16, v32, v16, u, vg, gates = _token_path_in(
        xs, mix_g, w_in[l], ln_g, ln_b, d_attn, d_gmlp, _row_tile(ms, TM_INPROJ), F32, "kv")
    q5 = q.reshape(bs, ts, n_heads, 2, HEAD_DIM).transpose(0, 2, 3, 1, 4)
    eye = jnp.eye(2, dtype=BF16)
    q_rows = (q5[:, :, :, :, None, :] * eye[None, None, :, None, :, None]).reshape(
        bs, n_heads * 2 * ts, V_DIM)
    k_new = k16.reshape(bs, ts * n_heads, V_DIM)
    v_new = v16.reshape(bs, ts * n_heads, V_DIM)
    ck = cache_k[l].reshape(-1, page * n_heads, V_DIM)
    cv = cache_v[l].reshape(-1, page * n_heads, V_DIM)
    n_pages = page_table.shape[1]
    pages = math.gcd(n_pages, DECODE_PAGES)
    att_s = _attn_decode(q_rows, k_new, v_new, ck, cv, page_table, lams, sub_g, lam_init, ts,
                         n_heads, pages)
    att_s = att_s.reshape(ms, d_attn)
    reps = CHUNK // ts
    wsp_s = jnp.tile(w_spatial[l][:, :ts, :ts], (1, reps, reps))
    bsp_s = jnp.tile(b_spatial[l][:, :ts], (1, reps)).reshape(n_groups, CHUNK, 1)
    ys = _token_path_out(xs, att_s, u, vg, gates, wsp_s, bsp_s, ts, wa, wb, wo, ffn_g,
                         w1, w2, fin_g, _row_tile(ms, TM_MIXER), _row_tile(ms, TM_FFN))

    return (yp.reshape(bp, sp, d_model), ys.reshape(bs, ts, d_model), new_k_prompt, new_v_prompt,
            k32.reshape(1, bs, ts, n_heads, V_DIM), v32.reshape(1, bs, ts, n_heads, V_DIM),
            vg.reshape(1, bs, ts, d_gmlp))
```

```python
import functools
import math

import jax
import jax.numpy as jnp
from jax import lax
from jax.experimental import pallas as pl
from jax.experimental.pallas import tpu as pltpu

F32 = jnp.float32
BF16 = jnp.bfloat16

HEAD_DIM = 64
V_DIM = 2 * HEAD_DIM
CHUNK = 128
NEG_INF = -1e30
LOG2_E = math.log2(math.e)
RMS_EPS = 1e-6
LN_EPS = 1e-5
MIB = 1024 * 1024
V7X_VMEM_BUDGET = 56 * MIB
TM_QNORM = 512
TM_INPROJ = 1024
TM_MIXER = 256
TM_FFN = 512


def _row_tile(m, target):
    t = min(m, target)
    assert m % t == 0
    return t


def _params(sem, vmem=V7X_VMEM_BUDGET):
    return pltpu.CompilerParams(dimension_semantics=sem, vmem_limit_bytes=vmem)


def _rms(x, g):
    return x * lax.rsqrt(jnp.mean(x * x, axis=-1, keepdims=True) + RMS_EPS) * g


def _lambda(lq1, lk1, lq2, lk2, lam_init):
    a = jnp.exp(jnp.sum(lq1[...] * lk1[...], axis=1, keepdims=True))
    b = jnp.exp(jnp.sum(lq2[...] * lk2[...], axis=1, keepdims=True))
    return a - b + lam_init


def _inproj_kernel(h_ref, w_ref, *refs, kind, scale):
    wbf_ref = refs[-1]

    @pl.when(pl.program_id(1) == 0)
    def _cast_weights():
        wbf_ref[...] = w_ref[...].astype(BF16)

    if kind == "q":
        h = _rms(h_ref[...], refs[0][...]).astype(BF16)
        refs[2][...] = h
        z = jnp.dot(h, wbf_ref[...], preferred_element_type=F32)
        refs[1][...] = (z * scale).astype(BF16)
        return
    z = jnp.dot(h_ref[...], wbf_ref[...], preferred_element_type=F32)
    if kind == "kv":
        refs[0][...] = z
        refs[1][...] = z.astype(BF16)
    elif kind == "vt":
        refs[0][...] = z
        tk = refs[1].shape[2]
        for c in range(refs[1].shape[0]):
            refs[1][c] = z[c * tk:(c + 1) * tk].T.astype(BF16)
    elif kind == "u":
        refs[0][...] = jax.nn.gelu(z).astype(refs[0].dtype)
    elif kind == "vs":
        g_ref, b_ref, o_ref = refs[0], refs[1], refs[2]
        y = jax.nn.gelu(z)
        mu = jnp.mean(y, axis=-1, keepdims=True)
        yc = y - mu
        var = jnp.mean(yc * yc, axis=-1, keepdims=True)
        o_ref[...] = (yc * lax.rsqrt(var + LN_EPS) * g_ref[...] + b_ref[...]).astype(o_ref.dtype)
    elif kind == "gates":
        refs[0][...] = jax.nn.sigmoid(z).astype(refs[0].dtype)
    else:
        raise ValueError(kind)


def _inproj(h, w, col0, ncols, kind, out_dtypes, tm, extra=(), scale=1.0, tn=1024):
    m, d = h.shape
    assert col0 % tn == 0 and ncols % tn == 0
    j0 = col0 // tn
    in_specs = [pl.BlockSpec((tm, d), lambda j, i: (i, 0)),
                pl.BlockSpec((d, tn), lambda j, i: (0, j0 + j))]
    in_specs += [pl.BlockSpec((1, e.shape[1]), lambda j, i: (0, 0)) for e in extra]
    out_shape = [jax.ShapeDtypeStruct((m, ncols), dt) for dt in out_dtypes]
    out_specs = [pl.BlockSpec((tm, tn), lambda j, i: (i, j)) for _ in out_dtypes]
    if kind == "q":
        assert ncols == tn
        out_shape[1] = jax.ShapeDtypeStruct((m, d), out_dtypes[1])
        out_specs[1] = pl.BlockSpec((tm, d), lambda j, i: (i, 0))
    if kind == "vt":
        tk = min(tm, ATTN_TK)
        out_shape[1] = jax.ShapeDtypeStruct((m // tk, ncols, tk), out_dtypes[1])
        out_specs[1] = pl.BlockSpec((tm // tk, tn, tk), lambda j, i: (i, j, 0))
    outs = pl.pallas_call(
        functools.partial(_inproj_kernel, kind=kind, scale=scale),
        out_shape=out_shape,
        grid=(ncols // tn, m // tm),
        in_specs=in_specs,
        out_specs=out_specs,
        scratch_shapes=[pltpu.VMEM((d, tn), BF16)],
        compiler_params=_params(("arbitrary", "arbitrary")),
        name="inproj_" + kind,
    )(h, w, *extra)
    return outs


ATTN_TK = 512
ATTN_TQ = 1024
L_ROWS = 16
LANE_GROUP = 512


def _attn_prompt_kernel(lq1, lk1, lq2, lk2, sg_ref, q_ref, k_ref, vt_ref, *rest, tq, tk, lam_init,
                        n_cast):
    cast_in = rest[:n_cast]
    o_ref = rest[n_cast]
    cast_out = rest[n_cast + 1:2 * n_cast + 1]
    m_ref, acc_ref, s_ref = rest[2 * n_cast + 1:]
    for src, dst in zip(cast_in, cast_out):
        dst[...] = src[...].astype(dst.dtype)

    qi = pl.program_id(1)
    q = q_ref[...]
    lane = lax.broadcasted_iota(jnp.int32, q.shape, 1)
    zero = jnp.zeros_like(q)
    qb = jnp.concatenate([jnp.where(lane < HEAD_DIM, q, zero),
                          jnp.where(lane >= HEAD_DIM, q, zero)], axis=0)
    qbt = qb.astype(F32).T.astype(BF16)
    ones = jnp.ones((L_ROWS, tk), BF16)

    m_ref[...] = jnp.full(m_ref.shape, NEG_INF, F32)
    acc_ref[...] = jnp.zeros(acc_ref.shape, F32)

    n_groups = 2 * tq // LANE_GROUP
    lanes = [slice(g * LANE_GROUP, (g + 1) * LANE_GROUP) for g in range(n_groups)]
    n_full = qi * (tq // tk)

    def scores(ki, g):
        k = k_ref[pl.ds(pl.multiple_of(ki * tk, tk), tk), :]
        return jnp.dot(k, qbt[:, lanes[g]], preferred_element_type=F32)

    def first_query(g):
        return (g * LANE_GROUP) % tq

    def causal(s, g, j):
        key = lax.broadcasted_iota(jnp.int32, s.shape, 0) + j * tk
        qry = lax.broadcasted_iota(jnp.int32, s.shape, 1) + first_query(g)
        return jnp.where(key <= qry, s, NEG_INF)

    def values(ki):
        return jnp.concatenate([vt_ref[ki], ones], axis=0)

    def softmax_pv(s, g, vt):
        ls = lanes[g]
        m_old = m_ref[:, ls]
        m_new = jnp.maximum(m_old, jnp.max(s, axis=0, keepdims=True))
        alpha = jnp.exp2(m_old - m_new)
        p = jnp.exp2(s - m_new).astype(BF16)
        acc_ref[:, ls] = alpha * acc_ref[:, ls] + jnp.dot(vt, p, preferred_element_type=F32)
        m_ref[:, ls] = m_new

    s_ref[...] = scores(0, 0)

    def body(ki, carry):
        vt = values(ki)
        s = s_ref[...]
        for g in range(n_groups):
            if g + 1 < n_groups:
                s_next = scores(ki, g + 1)
            else:
                s_ref[...] = scores(ki + 1, 0)
            softmax_pv(s, g, vt)
            s = s_next
        return carry

    lax.fori_loop(0, n_full, body, 0)

    pairs = [(j, g) for j in range(tq // tk) for g in range(n_groups)
             if first_query(g) + LANE_GROUP - 1 >= j * tk]
    vts = [values(n_full + j) for j in range(tq // tk)]

    def masked_scores(j, g):
        s = s_ref[...] if (j == 0 and g == 0) else scores(n_full + j, g)
        fully_visible = first_query(g) >= (j + 1) * tk - 1
        return s if fully_visible else causal(s, g, j)

    s = masked_scores(*pairs[0])
    for idx, (j, g) in enumerate(pairs):
        if idx + 1 < len(pairs):
            s_next = masked_scores(*pairs[idx + 1])
        softmax_pv(s, g, vts[j])
        s = s_next

    acc = acc_ref[...]
    n = acc[:V_DIM] / acc[V_DIM:V_DIM + 1]
    lam = _lambda(lq1, lk1, lq2, lk2, lam_init)
    att = n[:, :tq] - lam * n[:, tq:]
    y = att * lax.rsqrt(jnp.mean(att * att, axis=0, keepdims=True) + RMS_EPS) * sg_ref[...]
    o_ref[...] = (y * (1.0 - lam_init)).T.astype(o_ref.dtype)


def _attn_prompt(q, k, vt, lams, subln_g_col, lam_init, tq, cast_weights):
    s, w = q.shape
    n_heads = w // V_DIM
    tk = vt.shape[2]
    n_q = s // tq
    n_steps = n_heads * n_q
    assert vt.shape == (s // tk, w, tk) and tq % tk == 0 and s % tq == 0
    assert (2 * tq) % LANE_GROUP == 0 and 2 * tq >= 2 * LANE_GROUP and tq % LANE_GROUP == 0
    bf16_sublanes = 16
    assert all(cw.shape[0] % (n_steps * bf16_sublanes) == 0 for cw in cast_weights)
    lam_specs = [pl.BlockSpec((1, HEAD_DIM), lambda h, i: (0, 0)) for _ in lams]
    slab_specs = [pl.BlockSpec((cw.shape[0] // n_steps, cw.shape[1]), lambda h, i: (h * n_q + i, 0))
                  for cw in cast_weights]
    outs = pl.pallas_call(
        functools.partial(_attn_prompt_kernel, tq=tq, tk=tk, lam_init=lam_init,
                          n_cast=len(cast_weights)),
        out_shape=[jax.ShapeDtypeStruct((s, w), BF16)]
        + [jax.ShapeDtypeStruct(cw.shape, BF16) for cw in cast_weights],
        grid=(n_heads, n_q),
        in_specs=lam_specs + [
            pl.BlockSpec((V_DIM, 1), lambda h, i: (0, 0)),
            pl.BlockSpec((tq, V_DIM), lambda h, i: (i, h)),
            pl.BlockSpec((s, V_DIM), lambda h, i: (0, h)),
            pl.BlockSpec((s // tk, V_DIM, tk), lambda h, i: (0, h, 0)),
        ] + slab_specs,
        out_specs=[pl.BlockSpec((tq, V_DIM), lambda h, i: (i, h))] + slab_specs,
        scratch_shapes=[pltpu.VMEM((1, 2 * tq), F32), pltpu.VMEM((V_DIM + L_ROWS, 2 * tq), F32),
                        pltpu.VMEM((tk, LANE_GROUP), F32)],
        compiler_params=_params(("arbitrary", "arbitrary")),
        name="attn_prompt",
    )(*lams, subln_g_col, q, k, vt, *cast_weights)
    return outs[0], outs[1:]


DECODE_PAGES = 8
DECODE_SUB_PAGES = 2
DECODE_RING = 4

def _attn_decode_kernel(pt_ref, lq1, lk1, lq2, lk2, sg_ref, q_ref, kn_ref, vn_ref, ck_hbm, cv_hbm,
                        o_ref, m_ref, l_ref, acc_ref, kbuf, vbuf, sem,
                        *, pages, sub, t_new, n_heads, lam_init):
    n_buf = kbuf.shape[0]
    page_rows = kbuf.shape[1] // pages
    t = pl.program_id(1)
    steps_per_seq = pl.num_programs(1)
    g = pl.program_id(0) * steps_per_seq + t
    n_steps = pl.num_programs(0) * steps_per_seq

    def step_copies(step):
        slot = lax.rem(step, n_buf)
        out = []
        for i in range(pages):
            pg = pt_ref[step * pages + i]
            rows = pl.ds(i * page_rows, page_rows)
            out.append(pltpu.make_async_copy(ck_hbm.at[pg], kbuf.at[slot, rows], sem.at[0, slot]))
            out.append(pltpu.make_async_copy(cv_hbm.at[pg], vbuf.at[slot, rows], sem.at[1, slot]))
        return out

    @pl.when(g == 0)
    def _prime():
        for step in range(n_buf - 1):
            for c in step_copies(step):
                c.start()

    @pl.when(g + (n_buf - 1) < n_steps)
    def _prefetch():
        for c in step_copies(g + (n_buf - 1)):
            c.start()

    for c in step_copies(g):
        c.wait()
    slot = lax.rem(g, n_buf)

    q = q_ref[0]
    rows_per_head = 2 * t_new
    nt = (((1,), (1,)), ((), ()))

    def own_head(shape):
        row = lax.broadcasted_iota(jnp.int32, shape, 0)
        col = lax.broadcasted_iota(jnp.int32, shape, 1)
        return (col % n_heads) == (row // rows_per_head), row, col

    @pl.when(t == 0)
    def _init_from_new_tokens():
        s = lax.dot_general(q, kn_ref[0], nt, preferred_element_type=F32)
        same, row, col = own_head(s.shape)
        s = jnp.where(same & ((col // n_heads) <= (row % t_new)), s, NEG_INF)
        m = jnp.max(s, axis=1, keepdims=True)
        p = jnp.exp2(s - m)
        m_ref[...] = m
        l_ref[...] = jnp.sum(p, axis=1, keepdims=True)
        acc_ref[...] = jnp.dot(p.astype(BF16), vn_ref[0], preferred_element_type=F32)

    def page(buf, i):
        return buf[slot, pl.ds(i * page_rows, page_rows), :].astype(BF16)

    def scores(blk):
        s = jnp.concatenate(
            [lax.dot_general(q, page(kbuf, blk * sub + i), nt, preferred_element_type=F32)
             for i in range(sub)], axis=1)
        same, _, _ = own_head(s.shape)
        return jnp.where(same, s, NEG_INF)

    def softmax_pv(s, blk):
        m_old = m_ref[...]
        m_new = jnp.maximum(m_old, jnp.max(s, axis=1, keepdims=True))
        alpha = jnp.exp2(m_old - m_new)
        p = jnp.exp2(s - m_new)
        l_ref[...] = alpha * l_ref[...] + jnp.sum(p, axis=1, keepdims=True)
        pb = p.astype(BF16)
        pv = None
        for i in range(sub):
            d = jnp.dot(pb[:, i * page_rows:(i + 1) * page_rows], page(vbuf, blk * sub + i),
                        preferred_element_type=F32)
            pv = d if pv is None else pv + d
        acc_ref[...] = alpha * acc_ref[...] + pv
        m_ref[...] = m_new

    n_blocks = pages // sub
    s = scores(0)
    for blk in range(n_blocks):
        if blk + 1 < n_blocks:
            s_next = scores(blk + 1)
        softmax_pv(s, blk)
        s = s_next

    @pl.when(t == pl.num_programs(1) - 1)
    def _finalize():
        lam = _lambda(lq1, lk1, lq2, lk2, lam_init)
        n = acc_ref[...] / l_ref[...]
        for h in range(n_heads):
            r0 = h * rows_per_head
            att = n[r0:r0 + t_new] - lam * n[r0 + t_new:r0 + rows_per_head]
            o_ref[0, :, h * V_DIM:(h + 1) * V_DIM] = _rms(att, sg_ref[...]) * (1.0 - lam_init)


def _attn_decode(q, k_new, v_new, cache_k, cache_v, page_table, lams, subln_g, lam_init, t_new,
                 n_heads, pages):
    bs, r, _ = q.shape
    n_pages = page_table.shape[1]
    sub = math.gcd(pages, DECODE_SUB_PAGES)
    assert n_pages % pages == 0 and bs * (n_pages // pages) >= DECODE_RING - 1
    pt_flat = page_table.reshape(-1)
    page_rows = cache_k.shape[1]

    const = lambda b, t, pt: (0, 0)
    per_seq = lambda b, t, pt: (b, 0, 0)
    grid_spec = pltpu.PrefetchScalarGridSpec(
        num_scalar_prefetch=1,
        grid=(bs, n_pages // pages),
        in_specs=[pl.BlockSpec((1, HEAD_DIM), const) for _ in lams] + [
            pl.BlockSpec((1, V_DIM), const),
            pl.BlockSpec((1, r, V_DIM), per_seq),
            pl.BlockSpec((1,) + k_new.shape[1:], per_seq),
            pl.BlockSpec((1,) + v_new.shape[1:], per_seq),
            pl.BlockSpec(memory_space=pl.ANY),
            pl.BlockSpec(memory_space=pl.ANY),
        ],
        out_specs=pl.BlockSpec((1, t_new, n_heads * V_DIM), per_seq),
        scratch_shapes=[pltpu.VMEM((r, 1), F32), pltpu.VMEM((r, 1), F32), pltpu.VMEM((r, V_DIM), F32),
                        pltpu.VMEM((DECODE_RING, pages * page_rows, V_DIM), F32),
                        pltpu.VMEM((DECODE_RING, pages * page_rows, V_DIM), F32),
                        pltpu.SemaphoreType.DMA((2, DECODE_RING))],
    )
    return pl.pallas_call(
        functools.partial(_attn_decode_kernel, pages=pages, sub=sub, t_new=t_new, n_heads=n_heads,
                          lam_init=lam_init),
        out_shape=jax.ShapeDtypeStruct((bs, t_new, n_heads * V_DIM), F32),
        grid_spec=grid_spec,
        compiler_params=_params(("arbitrary", "arbitrary")),
        name="attn_decode",
    )(pt_flat, *lams, subln_g, q, k_new, v_new, cache_k, cache_v)


def _mixer_out_kernel(att_ref, u_ref, vg_ref, ga_ref, gb_ref, x_ref, wsp_ref, bsp_ref, wa_ref, wb_ref,
                      wo_ref, g_ref, y_ref, h_ref, ug_ref, *, chunk_len):
    tm = u_ref.shape[0]
    n_groups = wsp_ref.shape[0]
    row = lax.broadcasted_iota(jnp.int32, (CHUNK, CHUNK), 0)
    col = lax.broadcasted_iota(jnp.int32, (CHUNK, CHUNK), 1)
    mask = (row // chunk_len == col // chunk_len) & (col <= row)
    for g in range(n_groups):
        wg = jnp.where(mask, wsp_ref[g], 0.0).astype(BF16)
        bg = bsp_ref[g]
        cs = slice(g * CHUNK, (g + 1) * CHUNK)
        for r in range(tm // CHUNK):
            rs = slice(r * CHUNK, (r + 1) * CHUNK)
            sg = jnp.dot(wg, vg_ref[rs, cs].astype(BF16), preferred_element_type=F32) + bg
            ug_ref[rs, cs] = (u_ref[rs, cs].astype(F32) * sg).astype(BF16)
    a = jnp.dot(att_ref[...].astype(BF16), wa_ref[...], preferred_element_type=F32)
    gp = jnp.dot(ug_ref[...], wb_ref[...], preferred_element_type=F32)
    mm = (ga_ref[...].astype(F32) * a + gb_ref[...].astype(F32) * gp).astype(BF16)
    y = x_ref[...] + jnp.dot(mm, wo_ref[...], preferred_element_type=F32)
    y_ref[...] = y
    h_ref[...] = _rms(y, g_ref[...]).astype(h_ref.dtype)


def _mixer_out(att, u, vg, gates, x, wsp, bsp, wa, wb, wo, g, chunk_len, tm):
    m, d = x.shape
    d_in = att.shape[1]
    ng = wsp.shape[0]
    rows = lambda w: pl.BlockSpec((tm, w), lambda i: (i, 0))
    resident = lambda shape: pl.BlockSpec(shape, lambda i: (0,) * len(shape),
                                          pipeline_mode=pl.Buffered(1))
    return pl.pallas_call(
        functools.partial(_mixer_out_kernel, chunk_len=chunk_len),
        out_shape=[jax.ShapeDtypeStruct((m, d), F32), jax.ShapeDtypeStruct((m, d), BF16)],
        grid=(m // tm,),
        in_specs=[rows(d_in), rows(d_in), rows(d_in),
                  pl.BlockSpec((tm, d), lambda i: (i, 0)),
                  pl.BlockSpec((tm, d), lambda i: (i, 1)),
                  rows(d),
                  resident((ng, CHUNK, CHUNK)), resident((ng, CHUNK, 1)),
                  resident((d_in, d)), resident((d_in, d)), resident((d, d)), resident((1, d))],
        out_specs=[rows(d), rows(d)],
        scratch_shapes=[pltpu.VMEM((tm, d_in), BF16)],
        compiler_params=_params(("arbitrary",)),
        name="mixer_out",
    )(att, u, vg, gates, gates, x, wsp, bsp, wa, wb, wo, g)


def _ffn_kernel(h_ref, y_ref, w1_ref, w2_ref, g_ref, o_ref):
    f = pl.program_id(1)

    @pl.when(f == 0)
    def _init():
        o_ref[...] = y_ref[...]

    h = h_ref[...]
    half = w1_ref.shape[1] // 2
    a = [jnp.dot(h, w1_ref[:, c * half:(c + 1) * half], preferred_element_type=F32) for c in range(2)]
    part = None
    for c in range(2):
        r = jnp.square(jnp.maximum(a[c], 0.0)).astype(BF16)
        d = jnp.dot(r, w2_ref[c * half:(c + 1) * half, :], preferred_element_type=F32)
        part = d if part is None else part + d
    o_ref[...] += part

    @pl.when(f == pl.num_programs(1) - 1)
    def _final_norm():
        o_ref[...] = _rms(o_ref[...], g_ref[...])


def _ffn(h, y, w1, w2, g, tm, tf=1024):
    m, d = h.shape
    d_ff = w1.shape[1]
    return pl.pallas_call(
        _ffn_kernel,
        out_shape=jax.ShapeDtypeStruct((m, d), F32),
        grid=(m // tm, d_ff // tf),
        in_specs=[pl.BlockSpec((tm, d), lambda i, f: (i, 0)),
                  pl.BlockSpec((tm, d), lambda i, f: (i, 0)),
                  pl.BlockSpec((d, tf), lambda i, f: (0, f)),
                  pl.BlockSpec((tf, d), lambda i, f: (f, 0)),
                  pl.BlockSpec((1, d), lambda i, f: (0, 0))],
        out_specs=pl.BlockSpec((tm, d), lambda i, f: (i, 0)),
        compiler_params=_params(("arbitrary", "arbitrary")),
        name="ffn",
    )(h, y, w1, w2, g)


def _token_path_in(x, norm_g, w_in, ln_g, ln_b, d_attn, d_gmlp, tm, vg_dtype, v_kind):
    m, d_model = x.shape
    c = 0
    q, h = _inproj(x, w_in, c, d_attn, "q", [BF16, BF16], _row_tile(m, TM_QNORM), extra=(norm_g,),
                   scale=HEAD_DIM ** -0.5 * LOG2_E)
    c += d_attn
    k32, k16 = _inproj(h, w_in, c, d_attn, "kv", [F32, BF16], tm)
    c += d_attn
    v32, v16 = _inproj(h, w_in, c, d_attn, v_kind, [F32, BF16], tm)
    c += d_attn
    (u,) = _inproj(h, w_in, c, d_gmlp, "u", [BF16], tm)
    c += d_gmlp
    (vg,) = _inproj(h, w_in, c, d_gmlp, "vs", [vg_dtype], tm, extra=(ln_g, ln_b))
    c += d_gmlp
    (gates,) = _inproj(h, w_in, c, 2 * d_model, "gates", [BF16], tm)
    return q, k32, k16, v32, v16, u, vg, gates


def _token_path_out(x, att, u, vg, gates, wsp, bsp, chunk_len, wa, wb, wo, norm_ffn_g, w1, w2,
                    norm_final_g, tm_mix, tm_ffn):
    y, h2 = _mixer_out(att, u, vg, gates, x, wsp, bsp, wa, wb, wo, norm_ffn_g, chunk_len, tm_mix)
    return _ffn(h2, y, w1, w2, norm_final_g, tm_ffn)


def kernel(x_prompt, x_sample, cache_k, cache_v, page_table, norm_mix_g, w_in, lambda_q1, lambda_k1,
           lambda_q2, lambda_k2, subln_g, gmlp_ln_g, gmlp_ln_b, w_spatial, b_spatial, w_branch_a,
           w_branch_b, w_out, norm_ffn_g, w_ff1, w_ff2, norm_final_g):
    bp, sp, d_model = x_prompt.shape
    bs, ts, _ = x_sample.shape
    depth = w_in.shape[0]
    assert bp == 1 and depth == 1, "the final norm is fused into the MLP kernel of the single layer"
    n_heads = cache_k.shape[3]
    d_attn = n_heads * V_DIM
    d_gmlp = gmlp_ln_g.shape[1]
    n_groups = w_spatial.shape[1]
    page = cache_k.shape[2]
    assert sp % CHUNK == 0 and CHUNK % ts == 0 and (bs * ts) % CHUNK == 0

    l = 0
    lam_init = 0.8 - 0.6 * math.exp(-0.3 * l)
    row2 = lambda a: a.reshape(1, -1)
    lams = [row2(lambda_q1[l]), row2(lambda_k1[l]), row2(lambda_q2[l]), row2(lambda_k2[l])]
    sub_g = row2(subln_g[l])
    ln_g, ln_b = row2(gmlp_ln_g[l]), row2(gmlp_ln_b[l])
    mix_g, ffn_g, fin_g = row2(norm_mix_g[l]), row2(norm_ffn_g[l]), row2(norm_final_g)

    xp = x_prompt.reshape(sp, d_model)
    q, k32, k16, v32, v16t, u, vg, gates = _token_path_in(
        xp, mix_g, w_in[l], ln_g, ln_b, d_attn, d_gmlp, _row_tile(sp, TM_INPROJ), BF16, "vt")
    att, (wa, wb, wo, w1, w2) = _attn_prompt(
        q, k16, v16t, lams, subln_g[l].reshape(-1, 1), lam_init, _row_tile(sp, ATTN_TQ),
        [w_branch_a[l], w_branch_b[l], w_out[l], w_ff1[l], w_ff2[l]])
    bsp_p = b_spatial[l].reshape(n_groups, CHUNK, 1)
    yp = _token_path_out(xp, att, u, vg, gates, w_spatial[l], bsp_p, CHUNK, wa, wb, wo, ffn_g,
                         w1, w2, fin_g, _row_tile(sp, TM_MIXER), _row_tile(sp, TM_FFN))
    new_k_prompt = k32.reshape(1, bp, sp, n_heads, V_DIM)
    new_v_prompt = v32.reshape(1, bp, sp, n_heads, V_DIM)

    ms = bs * ts
    xs = x_sample.reshape(ms, d_model)
    q, k32, k16, v32, v16, u, vg, gates = _token_path_in(
        xs, mix_g, w_in[l], ln_g, ln_b, d_attn, d_gmlp, _row_tile(ms, TM_INPROJ), F32, "kv")
    q5 = q.reshape(bs, ts, n_heads, 2, HEAD_DIM).transpose(0, 2, 3, 1, 4)
    eye = jnp.eye(2, dtype=BF16)
    q_rows = (q5[:, :, :, :, None, :] * eye[None, None, :, None, :, None]).reshape(
        bs, n_heads * 2 * ts, V_DIM)
    k_new = k16.reshape(bs, ts * n_heads, V_DIM)
    v_new = v16.reshape(bs, ts * n_heads, V_DIM)
    ck = cache_k[l].reshape(-1, page * n_heads, V_DIM)
    cv = cache_v[l].reshape(-1, page * n_heads, V_DIM)
    n_pages = page_table.shape[1]
    pages = math.gcd(n_pages, DECODE_PAGES)
    att_s = _attn_decode(q_rows, k_new, v_new, ck, cv, page_table, lams, sub_g, lam_init, ts,
                         n_heads, pages)
    att_s = att_s.reshape(ms, d_attn)
    reps = CHUNK // ts
    wsp_s = jnp.tile(w_spatial[l][:, :ts, :ts], (1, reps, reps))
    bsp_s = jnp.tile(b_spatial[l][:, :ts], (1, reps)).reshape(n_groups, CHUNK, 1)
    ys = _token_path_out(xs, att_s, u, vg, gates, wsp_s, bsp_s, ts, wa, wb, wo, ffn_g,
                         w1, w2, fin_g, _row_tile(ms, TM_MIXER), _row_tile(ms, TM_FFN))

    return (yp.reshape(bp, sp, d_model), ys.reshape(bs, ts, d_model), new_k_prompt, new_v_prompt,
            k32.reshape(1, bs, ts, n_heads, V_DIM), v32.reshape(1, bs, ts, n_heads, V_DIM),
            vg.reshape(1, bs, ts, d_gmlp))
```

```python
import functools
import math

import jax
import jax.numpy as jnp
from jax import lax
from jax.experimental import pallas as pl
from jax.experimental.pallas import tpu as pltpu

F32 = jnp.float32
BF16 = jnp.bfloat16

HEAD_DIM = 64
V_DIM = 2 * HEAD_DIM
CHUNK = 128
NEG_INF = -1e30
LOG2_E = math.log2(math.e)
RMS_EPS = 1e-6
LN_EPS = 1e-5
MIB = 1024 * 1024
V7X_VMEM_BUDGET = 56 * MIB
V7X_VMEM_BUDGET_FFN = 60 * MIB
TM_QNORM = 512
TM_INPROJ = 1024
TM_MIXER = 256
TM_FFN = 512


def _row_tile(m, target):
    t = min(m, target)
    assert m % t == 0
    return t


def _params(sem, vmem=V7X_VMEM_BUDGET):
    return pltpu.CompilerParams(dimension_semantics=sem, vmem_limit_bytes=vmem)


def _rms(x, g):
    return x * lax.rsqrt(jnp.mean(x * x, axis=-1, keepdims=True) + RMS_EPS) * g


def _lambda(lq1, lk1, lq2, lk2, lam_init):
    a = jnp.exp(jnp.sum(lq1[...] * lk1[...], axis=1, keepdims=True))
    b = jnp.exp(jnp.sum(lq2[...] * lk2[...], axis=1, keepdims=True))
    return a - b + lam_init


def _inproj_kernel(h_ref, w_ref, *refs, kind, scale):
    wbf_ref = refs[-1]

    @pl.when(pl.program_id(1) == 0)
    def _cast_weights():
        wbf_ref[...] = w_ref[...].astype(BF16)

    if kind == "q":
        h = _rms(h_ref[...], refs[0][...]).astype(BF16)
        refs[2][...] = h
        z = jnp.dot(h, wbf_ref[...], preferred_element_type=F32)
        refs[1][...] = (z * scale).astype(BF16)
        return
    z = jnp.dot(h_ref[...], wbf_ref[...], preferred_element_type=F32)
    if kind == "kv":
        refs[0][...] = z
        refs[1][...] = z.astype(BF16)
    elif kind == "vt":
        refs[0][...] = z
        tk = refs[1].shape[2]
        for c in range(refs[1].shape[0]):
            refs[1][c] = z[c * tk:(c + 1) * tk].T.astype(BF16)
    elif kind == "u":
        refs[0][...] = jax.nn.gelu(z).astype(refs[0].dtype)
    elif kind == "vs":
        g_ref, b_ref, o_ref = refs[0], refs[1], refs[2]
        y = jax.nn.gelu(z)
        mu = jnp.mean(y, axis=-1, keepdims=True)
        yc = y - mu
        var = jnp.mean(yc * yc, axis=-1, keepdims=True)
        o_ref[...] = (yc * lax.rsqrt(var + LN_EPS) * g_ref[...] + b_ref[...]).astype(o_ref.dtype)
    elif kind == "gates":
        refs[0][...] = jax.nn.sigmoid(z).astype(refs[0].dtype)
    else:
        raise ValueError(kind)


def _inproj(h, w, col0, ncols, kind, out_dtypes, tm, extra=(), scale=1.0, tn=1024):
    m, d = h.shape
    assert col0 % tn == 0 and ncols % tn == 0
    j0 = col0 // tn
    in_specs = [pl.BlockSpec((tm, d), lambda j, i: (i, 0)),
                pl.BlockSpec((d, tn), lambda j, i: (0, j0 + j))]
    in_specs += [pl.BlockSpec((1, e.shape[1]), lambda j, i: (0, 0)) for e in extra]
    out_shape = [jax.ShapeDtypeStruct((m, ncols), dt) for dt in out_dtypes]
    out_specs = [pl.BlockSpec((tm, tn), lambda j, i: (i, j)) for _ in out_dtypes]
    if kind == "q":
        assert ncols == tn
        out_shape[1] = jax.ShapeDtypeStruct((m, d), out_dtypes[1])
        out_specs[1] = pl.BlockSpec((tm, d), lambda j, i: (i, 0))
    if kind == "vt":
        tk = min(tm, ATTN_TK)
        out_shape[1] = jax.ShapeDtypeStruct((m // tk, ncols, tk), out_dtypes[1])
        out_specs[1] = pl.BlockSpec((tm // tk, tn, tk), lambda j, i: (i, j, 0))
    outs = pl.pallas_call(
        functools.partial(_inproj_kernel, kind=kind, scale=scale),
        out_shape=out_shape,
        grid=(ncols // tn, m // tm),
        in_specs=in_specs,
        out_specs=out_specs,
        scratch_shapes=[pltpu.VMEM((d, tn), BF16)],
        compiler_params=_params(("arbitrary", "arbitrary")),
        name="inproj_" + kind,
    )(h, w, *extra)
    return outs


ATTN_TK = 512
ATTN_TQ = 2048
L_ROWS = 16
LANE_GROUP = 512


def _attn_prompt_kernel(lq1, lk1, lq2, lk2, sg_ref, q_ref, k_ref, vt_ref, *rest, tq, tk, lam_init,
                        n_cast):
    cast_in = rest[:n_cast]
    o_ref = rest[n_cast]
    cast_out = rest[n_cast + 1:2 * n_cast + 1]
    m_ref, acc_ref, s_ref = rest[2 * n_cast + 1:]
    for src, dst in zip(cast_in, cast_out):
        dst[...] = src[...].astype(dst.dtype)

    qi = pl.program_id(1)
    q = q_ref[...]
    lane = lax.broadcasted_iota(jnp.int32, q.shape, 1)
    zero = jnp.zeros_like(q)
    qb = jnp.concatenate([jnp.where(lane < HEAD_DIM, q, zero),
                          jnp.where(lane >= HEAD_DIM, q, zero)], axis=0)
    qbt = qb.astype(F32).T.astype(BF16)
    ones = jnp.ones((L_ROWS, tk), BF16)

    m_ref[...] = jnp.full(m_ref.shape, NEG_INF, F32)
    acc_ref[...] = jnp.zeros(acc_ref.shape, F32)

    n_groups = 2 * tq // LANE_GROUP
    lanes = [slice(g * LANE_GROUP, (g + 1) * LANE_GROUP) for g in range(n_groups)]
    n_full = qi * (tq // tk)

    def scores(ki, g):
        k = k_ref[pl.ds(pl.multiple_of(ki * tk, tk), tk), :]
        return jnp.dot(k, qbt[:, lanes[g]], preferred_element_type=F32)

    def first_query(g):
        return (g * LANE_GROUP) % tq

    def causal(s, g, j):
        key = lax.broadcasted_iota(jnp.int32, s.shape, 0) + j * tk
        qry = lax.broadcasted_iota(jnp.int32, s.shape, 1) + first_query(g)
        return jnp.where(key <= qry, s, NEG_INF)

    def values(ki):
        return jnp.concatenate([vt_ref[ki], ones], axis=0)

    def softmax_pv(s, g, vt):
        ls = lanes[g]
        m_old = m_ref[:, ls]
        m_new = jnp.maximum(m_old, jnp.max(s, axis=0, keepdims=True))
        alpha = jnp.exp2(m_old - m_new)
        p = jnp.exp2(s - m_new).astype(BF16)
        acc_ref[:, ls] = alpha * acc_ref[:, ls] + jnp.dot(vt, p, preferred_element_type=F32)
        m_ref[:, ls] = m_new

    s_ref[...] = scores(0, 0)

    def body(ki, carry):
        vt = values(ki)
        s = s_ref[...]
        for g in range(n_groups):
            if g + 1 < n_groups:
                s_next = scores(ki, g + 1)
            else:
                s_ref[...] = scores(ki + 1, 0)
            softmax_pv(s, g, vt)
            s = s_next
        return carry

    lax.fori_loop(0, n_full, body, 0)

    pairs = [(j, g) for j in range(tq // tk) for g in range(n_groups)
             if first_query(g) + LANE_GROUP - 1 >= j * tk]
    vts = [values(n_full + j) for j in range(tq // tk)]

    def masked_scores(j, g):
        s = s_ref[...] if (j == 0 and g == 0) else scores(n_full + j, g)
        fully_visible = first_query(g) >= (j + 1) * tk - 1
        return s if fully_visible else causal(s, g, j)

    s = masked_scores(*pairs[0])
    for idx, (j, g) in enumerate(pairs):
        if idx + 1 < len(pairs):
            s_next = masked_scores(*pairs[idx + 1])
        softmax_pv(s, g, vts[j])
        s = s_next

    acc = acc_ref[...]
    n = acc[:V_DIM] / acc[V_DIM:V_DIM + 1]
    lam = _lambda(lq1, lk1, lq2, lk2, lam_init)
    att = n[:, :tq] - lam * n[:, tq:]
    y = att * lax.rsqrt(jnp.mean(att * att, axis=0, keepdims=True) + RMS_EPS) * sg_ref[...]
    o_ref[...] = (y * (1.0 - lam_init)).T.astype(o_ref.dtype)


def _attn_prompt(q, k, vt, lams, subln_g_col, lam_init, tq, cast_weights):
    s, w = q.shape
    n_heads = w // V_DIM
    tk = vt.shape[2]
    n_q = s // tq
    n_steps = n_heads * n_q
    assert vt.shape == (s // tk, w, tk) and tq % tk == 0 and s % tq == 0
    assert (2 * tq) % LANE_GROUP == 0 and 2 * tq >= 2 * LANE_GROUP and tq % LANE_GROUP == 0
    bf16_sublanes = 16
    assert all(cw.shape[0] % (n_steps * bf16_sublanes) == 0 for cw in cast_weights)
    lam_specs = [pl.BlockSpec((1, HEAD_DIM), lambda h, i: (0, 0)) for _ in lams]
    slab_specs = [pl.BlockSpec((cw.shape[0] // n_steps, cw.shape[1]), lambda h, i: (h * n_q + i, 0))
                  for cw in cast_weights]
    outs = pl.pallas_call(
        functools.partial(_attn_prompt_kernel, tq=tq, tk=tk, lam_init=lam_init,
                          n_cast=len(cast_weights)),
        out_shape=[jax.ShapeDtypeStruct((s, w), BF16)]
        + [jax.ShapeDtypeStruct(cw.shape, BF16) for cw in cast_weights],
        grid=(n_heads, n_q),
        in_specs=lam_specs + [
            pl.BlockSpec((V_DIM, 1), lambda h, i: (0, 0)),
            pl.BlockSpec((tq, V_DIM), lambda h, i: (i, h)),
            pl.BlockSpec((s, V_DIM), lambda h, i: (0, h)),
            pl.BlockSpec((s // tk, V_DIM, tk), lambda h, i: (0, h, 0)),
        ] + slab_specs,
        out_specs=[pl.BlockSpec((tq, V_DIM), lambda h, i: (i, h))] + slab_specs,
        scratch_shapes=[pltpu.VMEM((1, 2 * tq), F32), pltpu.VMEM((V_DIM + L_ROWS, 2 * tq), F32),
                        pltpu.VMEM((tk, LANE_GROUP), F32)],
        compiler_params=_params(("arbitrary", "arbitrary")),
        name="attn_prompt",
    )(*lams, subln_g_col, q, k, vt, *cast_weights)
    return outs[0], outs[1:]


DECODE_PAGES = 8
DECODE_SUB_PAGES = 2
DECODE_RING = 4

def _attn_decode_kernel(pt_ref, lq1, lk1, lq2, lk2, sg_ref, q_ref, kn_ref, vn_ref, ck_hbm, cv_hbm,
                        o_ref, m_ref, l_ref, acc_ref, kbuf, vbuf, sem,
                        *, pages, sub, t_new, n_heads, lam_init):
    n_buf = kbuf.shape[0]
    page_rows = kbuf.shape[1] // pages
    t = pl.program_id(1)
    steps_per_seq = pl.num_programs(1)
    g = pl.program_id(0) * steps_per_seq + t
    n_steps = pl.num_programs(0) * steps_per_seq

    def step_copies(step):
        slot = lax.rem(step, n_buf)
        out = []
        for i in range(pages):
            pg = pt_ref[step * pages + i]
            rows = pl.ds(i * page_rows, page_rows)
            out.append(pltpu.make_async_copy(ck_hbm.at[pg], kbuf.at[slot, rows], sem.at[0, slot]))
            out.append(pltpu.make_async_copy(cv_hbm.at[pg], vbuf.at[slot, rows], sem.at[1, slot]))
        return out

    @pl.when(g == 0)
    def _prime():
        for step in range(n_buf - 1):
            for c in step_copies(step):
                c.start()

    @pl.when(g + (n_buf - 1) < n_steps)
    def _prefetch():
        for c in step_copies(g + (n_buf - 1)):
            c.start()

    for c in step_copies(g):
        c.wait()
    slot = lax.rem(g, n_buf)

    q = q_ref[0]
    rows_per_head = 2 * t_new
    nt = (((1,), (1,)), ((), ()))

    def own_head(shape):
        row = lax.broadcasted_iota(jnp.int32, shape, 0)
        col = lax.broadcasted_iota(jnp.int32, shape, 1)
        return (col % n_heads) == (row // rows_per_head), row, col

    @pl.when(t == 0)
    def _init_from_new_tokens():
        s = lax.dot_general(q, kn_ref[0], nt, preferred_element_type=F32)
        same, row, col = own_head(s.shape)
        s = jnp.where(same & ((col // n_heads) <= (row % t_new)), s, NEG_INF)
        m = jnp.max(s, axis=1, keepdims=True)
        p = jnp.exp2(s - m)
        m_ref[...] = m
        l_ref[...] = jnp.sum(p, axis=1, keepdims=True)
        acc_ref[...] = jnp.dot(p.astype(BF16), vn_ref[0], preferred_element_type=F32)

    def page(buf, i):
        return buf[slot, pl.ds(i * page_rows, page_rows), :].astype(BF16)

    def scores(blk):
        s = jnp.concatenate(
            [lax.dot_general(q, page(kbuf, blk * sub + i), nt, preferred_element_type=F32)
             for i in range(sub)], axis=1)
        same, _, _ = own_head(s.shape)
        return jnp.where(same, s, NEG_INF)

    def softmax_pv(s, blk):
        m_old = m_ref[...]
        m_new = jnp.maximum(m_old, jnp.max(s, axis=1, keepdims=True))
        alpha = jnp.exp2(m_old - m_new)
        p = jnp.exp2(s - m_new)
        l_ref[...] = alpha * l_ref[...] + jnp.sum(p, axis=1, keepdims=True)
        pb = p.astype(BF16)
        pv = None
        for i in range(sub):
            d = jnp.dot(pb[:, i * page_rows:(i + 1) * page_rows], page(vbuf, blk * sub + i),
                        preferred_element_type=F32)
            pv = d if pv is None else pv + d
        acc_ref[...] = alpha * acc_ref[...] + pv
        m_ref[...] = m_new

    n_blocks = pages // sub
    s = scores(0)
    for blk in range(n_blocks):
        if blk + 1 < n_blocks:
            s_next = scores(blk + 1)
        softmax_pv(s, blk)
        s = s_next

    @pl.when(t == pl.num_programs(1) - 1)
    def _finalize():
        lam = _lambda(lq1, lk1, lq2, lk2, lam_init)
        n = acc_ref[...] / l_ref[...]
        for h in range(n_heads):
            r0 = h * rows_per_head
            att = n[r0:r0 + t_new] - lam * n[r0 + t_new:r0 + rows_per_head]
            o_ref[0, :, h * V_DIM:(h + 1) * V_DIM] = _rms(att, sg_ref[...]) * (1.0 - lam_init)


def _attn_decode(q, k_new, v_new, cache_k, cache_v, page_table, lams, subln_g, lam_init, t_new,
                 n_heads, pages):
    bs, r, _ = q.shape
    n_pages = page_table.shape[1]
    sub = math.gcd(pages, DECODE_SUB_PAGES)
    assert n_pages % pages == 0 and bs * (n_pages // pages) >= DECODE_RING - 1
    pt_flat = page_table.reshape(-1)
    page_rows = cache_k.shape[1]

    const = lambda b, t, pt: (0, 0)
    per_seq = lambda b, t, pt: (b, 0, 0)
    grid_spec = pltpu.PrefetchScalarGridSpec(
        num_scalar_prefetch=1,
        grid=(bs, n_pages // pages),
        in_specs=[pl.BlockSpec((1, HEAD_DIM), const) for _ in lams] + [
            pl.BlockSpec((1, V_DIM), const),
            pl.BlockSpec((1, r, V_DIM), per_seq),
            pl.BlockSpec((1,) + k_new.shape[1:], per_seq),
            pl.BlockSpec((1,) + v_new.shape[1:], per_seq),
            pl.BlockSpec(memory_space=pl.ANY),
            pl.BlockSpec(memory_space=pl.ANY),
        ],
        out_specs=pl.BlockSpec((1, t_new, n_heads * V_DIM), per_seq),
        scratch_shapes=[pltpu.VMEM((r, 1), F32), pltpu.VMEM((r, 1), F32), pltpu.VMEM((r, V_DIM), F32),
                        pltpu.VMEM((DECODE_RING, pages * page_rows, V_DIM), F32),
                        pltpu.VMEM((DECODE_RING, pages * page_rows, V_DIM), F32),
                        pltpu.SemaphoreType.DMA((2, DECODE_RING))],
    )
    return pl.pallas_call(
        functools.partial(_attn_decode_kernel, pages=pages, sub=sub, t_new=t_new, n_heads=n_heads,
                          lam_init=lam_init),
        out_shape=jax.ShapeDtypeStruct((bs, t_new, n_heads * V_DIM), F32),
        grid_spec=grid_spec,
        compiler_params=_params(("arbitrary", "arbitrary")),
        name="attn_decode",
    )(pt_flat, *lams, subln_g, q, k_new, v_new, cache_k, cache_v)


def _mixer_out_kernel(att_ref, u_ref, vg_ref, ga_ref, gb_ref, x_ref, wsp_ref, bsp_ref, wa_ref, wb_ref,
                      wo_ref, g_ref, y_ref, h_ref, ug_ref, *, chunk_len):
    tm = u_ref.shape[0]
    n_groups = wsp_ref.shape[0]
    row = lax.broadcasted_iota(jnp.int32, (CHUNK, CHUNK), 0)
    col = lax.broadcasted_iota(jnp.int32, (CHUNK, CHUNK), 1)
    mask = (row // chunk_len == col // chunk_len) & (col <= row)
    for g in range(n_groups):
        wg = jnp.where(mask, wsp_ref[g], 0.0).astype(BF16)
        bg = bsp_ref[g]
        cs = slice(g * CHUNK, (g + 1) * CHUNK)
        for r in range(tm // CHUNK):
            rs = slice(r * CHUNK, (r + 1) * CHUNK)
            sg = jnp.dot(wg, vg_ref[rs, cs].astype(BF16), preferred_element_type=F32) + bg
            ug_ref[rs, cs] = (u_ref[rs, cs].astype(F32) * sg).astype(BF16)
    a = jnp.dot(att_ref[...].astype(BF16), wa_ref[...], preferred_element_type=F32)
    gp = jnp.dot(ug_ref[...], wb_ref[...], preferred_element_type=F32)
    mm = (ga_ref[...].astype(F32) * a + gb_ref[...].astype(F32) * gp).astype(BF16)
    y = x_ref[...] + jnp.dot(mm, wo_ref[...], preferred_element_type=F32)
    y_ref[...] = y
    h_ref[...] = _rms(y, g_ref[...]).astype(h_ref.dtype)


def _mixer_out(att, u, vg, gates, x, wsp, bsp, wa, wb, wo, g, chunk_len, tm):
    m, d = x.shape
    d_in = att.shape[1]
    ng = wsp.shape[0]
    rows = lambda w: pl.BlockSpec((tm, w), lambda i: (i, 0))
    resident = lambda shape: pl.BlockSpec(shape, lambda i: (0,) * len(shape),
                                          pipeline_mode=pl.Buffered(1))
    return pl.pallas_call(
        functools.partial(_mixer_out_kernel, chunk_len=chunk_len),
        out_shape=[jax.ShapeDtypeStruct((m, d), F32), jax.ShapeDtypeStruct((m, d), BF16)],
        grid=(m // tm,),
        in_specs=[rows(d_in), rows(d_in), rows(d_in),
                  pl.BlockSpec((tm, d), lambda i: (i, 0)),
                  pl.BlockSpec((tm, d), lambda i: (i, 1)),
                  rows(d),
                  resident((ng, CHUNK, CHUNK)), resident((ng, CHUNK, 1)),
                  resident((d_in, d)), resident((d_in, d)), resident((d, d)), resident((1, d))],
        out_specs=[rows(d), rows(d)],
        scratch_shapes=[pltpu.VMEM((tm, d_in), BF16)],
        compiler_params=_params(("arbitrary",)),
        name="mixer_out",
    )(att, u, vg, gates, gates, x, wsp, bsp, wa, wb, wo, g)


def _ffn_kernel(h_ref, y_ref, w1_ref, w2_ref, g_ref, o_ref):
    f = pl.program_id(1)

    @pl.when(f == 0)
    def _init():
        o_ref[...] = y_ref[...]

    h = h_ref[...]
    half = w1_ref.shape[1] // 2
    a = [jnp.dot(h, w1_ref[:, c * half:(c + 1) * half], preferred_element_type=F32) for c in range(2)]
    part = None
    for c in range(2):
        r = jnp.square(jnp.maximum(a[c], 0.0)).astype(BF16)
        d = jnp.dot(r, w2_ref[c * half:(c + 1) * half, :], preferred_element_type=F32)
        part = d if part is None else part + d
    o_ref[...] += part

    @pl.when(f == pl.num_programs(1) - 1)
    def _final_norm():
        o_ref[...] = _rms(o_ref[...], g_ref[...])


def _ffn(h, y, w1, w2, g, tm, tf=2048):
    m, d = h.shape
    d_ff = w1.shape[1]
    return pl.pallas_call(
        _ffn_kernel,
        out_shape=jax.ShapeDtypeStruct((m, d), F32),
        grid=(m // tm, d_ff // tf),
        in_specs=[pl.BlockSpec((tm, d), lambda i, f: (i, 0)),
                  pl.BlockSpec((tm, d), lambda i, f: (i, 0)),
                  pl.BlockSpec((d, tf), lambda i, f: (0, f)),
                  pl.BlockSpec((tf, d), lambda i, f: (f, 0)),
                  pl.BlockSpec((1, d), lambda i, f: (0, 0))],
        out_specs=pl.BlockSpec((tm, d), lambda i, f: (i, 0)),
        compiler_params=_params(("arbitrary", "arbitrary"), V7X_VMEM_BUDGET_FFN),
        name="ffn",
    )(h, y, w1, w2, g)


def _token_path_in(x, norm_g, w_in, ln_g, ln_b, d_attn, d_gmlp, tm, vg_dtype, v_kind):
    m, d_model = x.shape
    c = 0
    q, h = _inproj(x, w_in, c, d_attn, "q", [BF16, BF16], _row_tile(m, TM_QNORM), extra=(norm_g,),
                   scale=HEAD_DIM ** -0.5 * LOG2_E)
    c += d_attn
    k32, k16 = _inproj(h, w_in, c, d_attn, "kv", [F32, BF16], tm)
    c += d_attn
    v32, v16 = _inproj(h, w_in, c, d_attn, v_kind, [F32, BF16], tm)
    c += d_attn
    (u,) = _inproj(h, w_in, c, d_gmlp, "u", [BF16], tm)
    c += d_gmlp
    (vg,) = _inproj(h, w_in, c, d_gmlp, "vs", [vg_dtype], tm, extra=(ln_g, ln_b))
    c += d_gmlp
    (gates,) = _inproj(h, w_in, c, 2 * d_model, "gates", [BF16], tm)
    return q, k32, k16, v32, v16, u, vg, gates


def _token_path_out(x, att, u, vg, gates, wsp, bsp, chunk_len, wa, wb, wo, norm_ffn_g, w1, w2,
                    norm_final_g, tm_mix, tm_ffn):
    y, h2 = _mixer_out(att, u, vg, gates, x, wsp, bsp, wa, wb, wo, norm_ffn_g, chunk_len, tm_mix)
    return _ffn(h2, y, w1, w2, norm_final_g, tm_ffn)


def kernel(x_prompt, x_sample, cache_k, cache_v, page_table, norm_mix_g, w_in, lambda_q1, lambda_k1,
           lambda_q2, lambda_k2, subln_g, gmlp_ln_g, gmlp_ln_b, w_spatial, b_spatial, w_branch_a,
           w_branch_b, w_out, norm_ffn_g, w_ff1, w_ff2, norm_final_g):
    bp, sp, d_model = x_prompt.shape
    bs, ts, _ = x_sample.shape
    depth = w_in.shape[0]
    assert bp == 1 and depth == 1, "the final norm is fused into the MLP kernel of the single layer"
    n_heads = cache_k.shape[3]
    d_attn = n_heads * V_DIM
    d_gmlp = gmlp_ln_g.shape[1]
    n_groups = w_spatial.shape[1]
    page = cache_k.shape[2]
    assert sp % CHUNK == 0 and CHUNK % ts == 0 and (bs * ts) % CHUNK == 0

    l = 0
    lam_init = 0.8 - 0.6 * math.exp(-0.3 * l)
    row2 = lambda a: a.reshape(1, -1)
    lams = [row2(lambda_q1[l]), row2(lambda_k1[l]), row2(lambda_q2[l]), row2(lambda_k2[l])]
    sub_g = row2(subln_g[l])
    ln_g, ln_b = row2(gmlp_ln_g[l]), row2(gmlp_ln_b[l])
    mix_g, ffn_g, fin_g = row2(norm_mix_g[l]), row2(norm_ffn_g[l]), row2(norm_final_g)

    xp = x_prompt.reshape(sp, d_model)
    q, k32, k16, v32, v16t, u, vg, gates = _token_path_in(
        xp, mix_g, w_in[l], ln_g, ln_b, d_attn, d_gmlp, _row_tile(sp, TM_INPROJ), BF16, "vt")
    att, (wa, wb, wo, w1, w2) = _attn_prompt(
        q, k16, v16t, lams, subln_g[l].reshape(-1, 1), lam_init, _row_tile(sp, ATTN_TQ),
        [w_branch_a[l], w_branch_b[l], w_out[l], w_ff1[l], w_ff2[l]])
    bsp_p = b_spatial[l].reshape(n_groups, CHUNK, 1)
    yp = _token_path_out(xp, att, u, vg, gates, w_spatial[l], bsp_p, CHUNK, wa, wb, wo, ffn_g,
                         w1, w2, fin_g, _row_tile(sp, TM_MIXER), _row_tile(sp, TM_FFN))
    new_k_prompt = k32.reshape(1, bp, sp, n_heads, V_DIM)
    new_v_prompt = v32.reshape(1, bp, sp, n_heads, V_DIM)

    ms = bs * ts
    xs = x_sample.reshape(ms, d_model)
    q, k32, k16, v32, v16, u, vg, gates = _token_path_in(
        xs, mix_g, w_in[l], ln_g, ln_b, d_attn, d_gmlp, _row_tile(ms, TM_INPROJ), F32, "kv")
    q5 = q.reshape(bs, ts, n_heads, 2, HEAD_DIM).transpose(0, 2, 3, 1, 4)
    eye = jnp.eye(2, dtype=BF16)
    q_rows = (q5[:, :, :, :, None, :] * eye[None, None, :, None, :, None]).reshape(
        bs, n_heads * 2 * ts, V_DIM)
    k_new = k16.reshape(bs, ts * n_heads, V_DIM)
    v_new = v16.reshape(bs, ts * n_heads, V_DIM)
    ck = cache_k[l].reshape(-1, page * n_heads, V_DIM)
    cv = cache_v[l].reshape(-1, page * n_heads, V_DIM)
    n_pages = page_table.shape[1]
    pages = math.gcd(n_pages, DECODE_PAGES)
    att_s = _attn_decode(q_rows, k_new, v_new, ck, cv, page_table, lams, sub_g, lam_init, ts,
                         n_heads, pages)
    att_s = att_s.reshape(ms, d_attn)
    reps = CHUNK // ts
    wsp_s = jnp.tile(w_spatial[l][:, :ts, :ts], (1, reps, reps))
    bsp_s = jnp.tile(b_spatial[l][:, :ts], (1, reps)).reshape(n_groups, CHUNK, 1)
    ys = _token_path_out(xs, att_s, u, vg, gates, wsp_s, bsp_s, ts, wa, wb, wo, ffn_g,
                         w1, w2, fin_g, _row_tile(ms, TM_MIXER), _row_tile(ms, TM_FFN))

    return (yp.reshape(bp, sp, d_model), ys.reshape(bs, ts, d_model), new_k_prompt, new_v_prompt,
            k32.reshape(1, bs, ts, n_heads, V_DIM), v32.reshape(1, bs, ts, n_heads, V_DIM),
            vg.reshape(1, bs, ts, d_gmlp))
```

```python
import functools
import math

import jax
import jax.numpy as jnp
from jax import lax
from jax.experimental import pallas as pl
from jax.experimental.pallas import tpu as pltpu

F32 = jnp.float32
BF16 = jnp.bfloat16

HEAD_DIM = 64
V_DIM = 2 * HEAD_DIM
CHUNK = 128
NEG_INF = -1e30
LOG2_E = math.log2(math.e)
RMS_EPS = 1e-6
LN_EPS = 1e-5
MIB = 1024 * 1024
V7X_VMEM_BUDGET = 56 * MIB
V7X_VMEM_BUDGET_FFN = 60 * MIB
TM_QNORM = 512
TM_INPROJ = 1024
TM_MIXER = 256
TM_FFN = 512


def _row_tile(m, target):
    t = min(m, target)
    assert m % t == 0
    return t


def _params(sem, vmem=V7X_VMEM_BUDGET):
    return pltpu.CompilerParams(dimension_semantics=sem, vmem_limit_bytes=vmem)


def _rms(x, g):
    return x * lax.rsqrt(jnp.mean(x * x, axis=-1, keepdims=True) + RMS_EPS) * g


def _lambda(lq1, lk1, lq2, lk2, lam_init):
    a = jnp.exp(jnp.sum(lq1[...] * lk1[...], axis=1, keepdims=True))
    b = jnp.exp(jnp.sum(lq2[...] * lk2[...], axis=1, keepdims=True))
    return a - b + lam_init


def _inproj_kernel(h_ref, w_ref, *refs, kind, scale):
    wbf_ref = refs[-1]

    @pl.when(pl.program_id(1) == 0)
    def _cast_weights():
        wbf_ref[...] = w_ref[...].astype(BF16)

    if kind == "q":
        h = _rms(h_ref[...], refs[0][...]).astype(BF16)
        refs[2][...] = h
        z = jnp.dot(h, wbf_ref[...], preferred_element_type=F32)
        refs[1][...] = (z * scale).astype(BF16)
        return
    z = jnp.dot(h_ref[...], wbf_ref[...], preferred_element_type=F32)
    if kind == "kv":
        refs[0][...] = z
        refs[1][...] = z.astype(BF16)
    elif kind == "vt":
        refs[0][...] = z
        tk = refs[1].shape[2]
        for c in range(refs[1].shape[0]):
            refs[1][c] = z[c * tk:(c + 1) * tk].T.astype(BF16)
    elif kind == "u":
        refs[0][...] = jax.nn.gelu(z).astype(refs[0].dtype)
    elif kind == "vs":
        g_ref, b_ref, o_ref = refs[0], refs[1], refs[2]
        y = jax.nn.gelu(z)
        mu = jnp.mean(y, axis=-1, keepdims=True)
        yc = y - mu
        var = jnp.mean(yc * yc, axis=-1, keepdims=True)
        o_ref[...] = (yc * lax.rsqrt(var + LN_EPS) * g_ref[...] + b_ref[...]).astype(o_ref.dtype)
    elif kind == "gates":
        refs[0][...] = jax.nn.sigmoid(z).astype(refs[0].dtype)
    else:
        raise ValueError(kind)


def _inproj(h, w, col0, ncols, kind, out_dtypes, tm, extra=(), scale=1.0, tn=1024):
    m, d = h.shape
    assert col0 % tn == 0 and ncols % tn == 0
    j0 = col0 // tn
    in_specs = [pl.BlockSpec((tm, d), lambda j, i: (i, 0)),
                pl.BlockSpec((d, tn), lambda j, i: (0, j0 + j))]
    in_specs += [pl.BlockSpec((1, e.shape[1]), lambda j, i: (0, 0)) for e in extra]
    out_shape = [jax.ShapeDtypeStruct((m, ncols), dt) for dt in out_dtypes]
    out_specs = [pl.BlockSpec((tm, tn), lambda j, i: (i, j)) for _ in out_dtypes]
    if kind == "q":
        assert ncols == tn
        out_shape[1] = jax.ShapeDtypeStruct((m, d), out_dtypes[1])
        out_specs[1] = pl.BlockSpec((tm, d), lambda j, i: (i, 0))
    if kind == "vt":
        tk = min(tm, ATTN_TK)
        out_shape[1] = jax.ShapeDtypeStruct((m // tk, ncols, tk), out_dtypes[1])
        out_specs[1] = pl.BlockSpec((tm // tk, tn, tk), lambda j, i: (i, j, 0))
    outs = pl.pallas_call(
        functools.partial(_inproj_kernel, kind=kind, scale=scale),
        out_shape=out_shape,
        grid=(ncols // tn, m // tm),
        in_specs=in_specs,
        out_specs=out_specs,
        scratch_shapes=[pltpu.VMEM((d, tn), BF16)],
        compiler_params=_params(("arbitrary", "arbitrary")),
        name="inproj_" + kind,
    )(h, w, *extra)
    return outs


ATTN_TK = 512
ATTN_TQ = 2048
L_ROWS = 16
LANE_GROUP = 512


def _attn_prompt_kernel(lq1, lk1, lq2, lk2, sg_ref, q_ref, k_ref, vt_ref, *rest, tq, tk, lam_init,
                        n_cast):
    cast_in = rest[:n_cast]
    o_ref = rest[n_cast]
    cast_out = rest[n_cast + 1:2 * n_cast + 1]
    m_ref, acc_ref, s_ref = rest[2 * n_cast + 1:]
    for src, dst in zip(cast_in, cast_out):
        dst[...] = src[...].astype(dst.dtype)

    qi = pl.program_id(1)
    q = q_ref[...]
    lane = lax.broadcasted_iota(jnp.int32, q.shape, 1)
    zero = jnp.zeros_like(q)
    qb = jnp.concatenate([jnp.where(lane < HEAD_DIM, q, zero),
                          jnp.where(lane >= HEAD_DIM, q, zero)], axis=0)
    qbt = qb.astype(F32).T.astype(BF16)
    ones = jnp.ones((L_ROWS, tk), BF16)

    m_ref[...] = jnp.full(m_ref.shape, NEG_INF, F32)
    acc_ref[...] = jnp.zeros(acc_ref.shape, F32)

    n_groups = 2 * tq // LANE_GROUP
    lanes = [slice(g * LANE_GROUP, (g + 1) * LANE_GROUP) for g in range(n_groups)]
    n_full = qi * (tq // tk)

    def scores(ki, g):
        k = k_ref[pl.ds(pl.multiple_of(ki * tk, tk), tk), :]
        return jnp.dot(k, qbt[:, lanes[g]], preferred_element_type=F32)

    def first_query(g):
        return (g * LANE_GROUP) % tq

    def causal(s, g, j):
        key = lax.broadcasted_iota(jnp.int32, s.shape, 0) + j * tk
        qry = lax.broadcasted_iota(jnp.int32, s.shape, 1) + first_query(g)
        return jnp.where(key <= qry, s, NEG_INF)

    def values(ki):
        return jnp.concatenate([vt_ref[ki], ones], axis=0)

    def softmax_pv(s, g, vt):
        ls = lanes[g]
        m_old = m_ref[:, ls]
        m_new = jnp.maximum(m_old, jnp.max(s, axis=0, keepdims=True))
        alpha = jnp.exp2(m_old - m_new)
        p = jnp.exp2(s - m_new).astype(BF16)
        acc_ref[:, ls] = alpha * acc_ref[:, ls] + jnp.dot(vt, p, preferred_element_type=F32)
        m_ref[:, ls] = m_new

    s_ref[...] = scores(0, 0)

    def body(ki, carry):
        vt = values(ki)
        s = s_ref[...]
        for g in range(n_groups):
            if g + 1 < n_groups:
                s_next = scores(ki, g + 1)
            else:
                s_ref[...] = scores(ki + 1, 0)
            softmax_pv(s, g, vt)
            s = s_next
        return carry

    lax.fori_loop(0, n_full, body, 0)

    pairs = [(j, g) for j in range(tq // tk) for g in range(n_groups)
             if first_query(g) + LANE_GROUP - 1 >= j * tk]
    vts = [values(n_full + j) for j in range(tq // tk)]

    def masked_scores(j, g):
        s = s_ref[...] if (j == 0 and g == 0) else scores(n_full + j, g)
        fully_visible = first_query(g) >= (j + 1) * tk - 1
        return s if fully_visible else causal(s, g, j)

    s = masked_scores(*pairs[0])
    for idx, (j, g) in enumerate(pairs):
        if idx + 1 < len(pairs):
            s_next = masked_scores(*pairs[idx + 1])
        softmax_pv(s, g, vts[j])
        s = s_next

    acc = acc_ref[...]
    n = acc[:V_DIM] / acc[V_DIM:V_DIM + 1]
    lam = _lambda(lq1, lk1, lq2, lk2, lam_init)
    att = n[:, :tq] - lam * n[:, tq:]
    y = att * lax.rsqrt(jnp.mean(att * att, axis=0, keepdims=True) + RMS_EPS) * sg_ref[...]
    o_ref[...] = (y * (1.0 - lam_init)).T.astype(o_ref.dtype)


def _attn_prompt(q, k, vt, lams, subln_g_col, lam_init, tq, cast_weights):
    s, w = q.shape
    n_heads = w // V_DIM
    tk = vt.shape[2]
    n_q = s // tq
    n_steps = n_heads * n_q
    assert vt.shape == (s // tk, w, tk) and tq % tk == 0 and s % tq == 0
    assert (2 * tq) % LANE_GROUP == 0 and 2 * tq >= 2 * LANE_GROUP and tq % LANE_GROUP == 0
    bf16_sublanes = 16
    assert all(cw.shape[0] % (n_steps * bf16_sublanes) == 0 for cw in cast_weights)
    lam_specs = [pl.BlockSpec((1, HEAD_DIM), lambda h, i: (0, 0)) for _ in lams]
    slab_specs = [pl.BlockSpec((cw.shape[0] // n_steps, cw.shape[1]), lambda h, i: (h * n_q + i, 0))
                  for cw in cast_weights]
    outs = pl.pallas_call(
        functools.partial(_attn_prompt_kernel, tq=tq, tk=tk, lam_init=lam_init,
                          n_cast=len(cast_weights)),
        out_shape=[jax.ShapeDtypeStruct((s, w), BF16)]
        + [jax.ShapeDtypeStruct(cw.shape, BF16) for cw in cast_weights],
        grid=(n_heads, n_q),
        in_specs=lam_specs + [
            pl.BlockSpec((V_DIM, 1), lambda h, i: (0, 0)),
            pl.BlockSpec((tq, V_DIM), lambda h, i: (i, h)),
            pl.BlockSpec((s, V_DIM), lambda h, i: (0, h)),
            pl.BlockSpec((s // tk, V_DIM, tk), lambda h, i: (0, h, 0)),
        ] + slab_specs,
        out_specs=[pl.BlockSpec((tq, V_DIM), lambda h, i: (i, h))] + slab_specs,
        scratch_shapes=[pltpu.VMEM((1, 2 * tq), F32), pltpu.VMEM((V_DIM + L_ROWS, 2 * tq), F32),
                        pltpu.VMEM((tk, LANE_GROUP), F32)],
        compiler_params=_params(("arbitrary", "arbitrary")),
        name="attn_prompt",
    )(*lams, subln_g_col, q, k, vt, *cast_weights)
    return outs[0], outs[1:]


DECODE_PAGES = 8
DECODE_SUB_PAGES = 4
DECODE_RING = 4

def _attn_decode_kernel(pt_ref, lq1, lk1, lq2, lk2, sg_ref, q_ref, kn_ref, vn_ref, ck_hbm, cv_hbm,
                        o_ref, m_ref, l_ref, acc_ref, kbuf, vbuf, sem,
                        *, pages, sub, t_new, n_heads, lam_init):
    n_buf = kbuf.shape[0]
    page_rows = kbuf.shape[1] // pages
    t = pl.program_id(1)
    steps_per_seq = pl.num_programs(1)
    g = pl.program_id(0) * steps_per_seq + t
    n_steps = pl.num_programs(0) * steps_per_seq

    def step_copies(step):
        slot = lax.rem(step, n_buf)
        out = []
        for i in range(pages):
            pg = pt_ref[step * pages + i]
            rows = pl.ds(i * page_rows, page_rows)
            out.append(pltpu.make_async_copy(ck_hbm.at[pg], kbuf.at[slot, rows], sem.at[0, slot]))
            out.append(pltpu.make_async_copy(cv_hbm.at[pg], vbuf.at[slot, rows], sem.at[1, slot]))
        return out

    def start_step(step):
        for n, c in enumerate(step_copies(step)):
            c.start(priority=n % 2)

    @pl.when(g == 0)
    def _prime():
        for step in range(n_buf - 1):
            start_step(step)

    @pl.when(g + (n_buf - 1) < n_steps)
    def _prefetch():
        start_step(g + (n_buf - 1))

    for c in step_copies(g):
        c.wait()
    slot = lax.rem(g, n_buf)

    q = q_ref[0]
    rows_per_head = 2 * t_new
    nt = (((1,), (1,)), ((), ()))

    def own_head(shape):
        row = lax.broadcasted_iota(jnp.int32, shape, 0)
        col = lax.broadcasted_iota(jnp.int32, shape, 1)
        return (col % n_heads) == (row // rows_per_head), row, col

    @pl.when(t == 0)
    def _init_from_new_tokens():
        s = lax.dot_general(q, kn_ref[0], nt, preferred_element_type=F32)
        same, row, col = own_head(s.shape)
        s = jnp.where(same & ((col // n_heads) <= (row % t_new)), s, NEG_INF)
        m = jnp.max(s, axis=1, keepdims=True)
        p = jnp.exp2(s - m)
        m_ref[...] = m
        l_ref[...] = jnp.sum(p, axis=1, keepdims=True)
        acc_ref[...] = jnp.dot(p.astype(BF16), vn_ref[0], preferred_element_type=F32)

    def page(buf, i):
        return buf[slot, pl.ds(i * page_rows, page_rows), :].astype(BF16)

    def scores(blk):
        s = jnp.concatenate(
            [lax.dot_general(q, page(kbuf, blk * sub + i), nt, preferred_element_type=F32)
             for i in range(sub)], axis=1)
        same, _, _ = own_head(s.shape)
        return jnp.where(same, s, NEG_INF)

    def softmax_pv(s, blk):
        m_old = m_ref[...]
        m_new = jnp.maximum(m_old, jnp.max(s, axis=1, keepdims=True))
        alpha = jnp.exp2(m_old - m_new)
        p = jnp.exp2(s - m_new)
        l_ref[...] = alpha * l_ref[...] + jnp.sum(p, axis=1, keepdims=True)
        pb = p.astype(BF16)
        pv = None
        for i in range(sub):
            d = jnp.dot(pb[:, i * page_rows:(i + 1) * page_rows], page(vbuf, blk * sub + i),
                        preferred_element_type=F32)
            pv = d if pv is None else pv + d
        acc_ref[...] = alpha * acc_ref[...] + pv
        m_ref[...] = m_new

    n_blocks = pages // sub
    s = scores(0)
    for blk in range(n_blocks):
        if blk + 1 < n_blocks:
            s_next = scores(blk + 1)
        softmax_pv(s, blk)
        s = s_next

    @pl.when(t == pl.num_programs(1) - 1)
    def _finalize():
        lam = _lambda(lq1, lk1, lq2, lk2, lam_init)
        n = acc_ref[...] / l_ref[...]
        for h in range(n_heads):
            r0 = h * rows_per_head
            att = n[r0:r0 + t_new] - lam * n[r0 + t_new:r0 + rows_per_head]
            o_ref[0, :, h * V_DIM:(h + 1) * V_DIM] = _rms(att, sg_ref[...]) * (1.0 - lam_init)


def _attn_decode(q, k_new, v_new, cache_k, cache_v, page_table, lams, subln_g, lam_init, t_new,
                 n_heads, pages):
    bs, r, _ = q.shape
    n_pages = page_table.shape[1]
    sub = math.gcd(pages, DECODE_SUB_PAGES)
    assert n_pages % pages == 0 and bs * (n_pages // pages) >= DECODE_RING - 1
    pt_flat = page_table.reshape(-1)
    page_rows = cache_k.shape[1]

    const = lambda b, t, pt: (0, 0)
    per_seq = lambda b, t, pt: (b, 0, 0)
    grid_spec = pltpu.PrefetchScalarGridSpec(
        num_scalar_prefetch=1,
        grid=(bs, n_pages // pages),
        in_specs=[pl.BlockSpec((1, HEAD_DIM), const) for _ in lams] + [
            pl.BlockSpec((1, V_DIM), const),
            pl.BlockSpec((1, r, V_DIM), per_seq),
            pl.BlockSpec((1,) + k_new.shape[1:], per_seq),
            pl.BlockSpec((1,) + v_new.shape[1:], per_seq),
            pl.BlockSpec(memory_space=pl.ANY),
            pl.BlockSpec(memory_space=pl.ANY),
        ],
        out_specs=pl.BlockSpec((1, t_new, n_heads * V_DIM), per_seq),
        scratch_shapes=[pltpu.VMEM((r, 1), F32), pltpu.VMEM((r, 1), F32), pltpu.VMEM((r, V_DIM), F32),
                        pltpu.VMEM((DECODE_RING, pages * page_rows, V_DIM), F32),
                        pltpu.VMEM((DECODE_RING, pages * page_rows, V_DIM), F32),
                        pltpu.SemaphoreType.DMA((2, DECODE_RING))],
    )
    return pl.pallas_call(
        functools.partial(_attn_decode_kernel, pages=pages, sub=sub, t_new=t_new, n_heads=n_heads,
                          lam_init=lam_init),
        out_shape=jax.ShapeDtypeStruct((bs, t_new, n_heads * V_DIM), F32),
        grid_spec=grid_spec,
        compiler_params=_params(("arbitrary", "arbitrary")),
        name="attn_decode",
    )(pt_flat, *lams, subln_g, q, k_new, v_new, cache_k, cache_v)


def _mixer_out_kernel(att_ref, u_ref, vg_ref, ga_ref, gb_ref, x_ref, wsp_ref, bsp_ref, wa_ref, wb_ref,
                      wo_ref, g_ref, y_ref, h_ref, ug_ref, *, chunk_len):
    tm = u_ref.shape[0]
    n_groups = wsp_ref.shape[0]
    row = lax.broadcasted_iota(jnp.int32, (CHUNK, CHUNK), 0)
    col = lax.broadcasted_iota(jnp.int32, (CHUNK, CHUNK), 1)
    mask = (row // chunk_len == col // chunk_len) & (col <= row)
    for g in range(n_groups):
        wg = jnp.where(mask, wsp_ref[g], 0.0).astype(BF16)
        bg = bsp_ref[g]
        cs = slice(g * CHUNK, (g + 1) * CHUNK)
        for r in range(tm // CHUNK):
            rs = slice(r * CHUNK, (r + 1) * CHUNK)
            sg = jnp.dot(wg, vg_ref[rs, cs].astype(BF16), preferred_element_type=F32) + bg
            ug_ref[rs, cs] = (u_ref[rs, cs].astype(F32) * sg).astype(BF16)
    a = jnp.dot(att_ref[...].astype(BF16), wa_ref[...], preferred_element_type=F32)
    gp = jnp.dot(ug_ref[...], wb_ref[...], preferred_element_type=F32)
    mm = (ga_ref[...].astype(F32) * a + gb_ref[...].astype(F32) * gp).astype(BF16)
    y = x_ref[...] + jnp.dot(mm, wo_ref[...], preferred_element_type=F32)
    y_ref[...] = y
    h_ref[...] = _rms(y, g_ref[...]).astype(h_ref.dtype)


def _mixer_out(att, u, vg, gates, x, wsp, bsp, wa, wb, wo, g, chunk_len, tm):
    m, d = x.shape
    d_in = att.shape[1]
    ng = wsp.shape[0]
    rows = lambda w: pl.BlockSpec((tm, w), lambda i: (i, 0))
    resident = lambda shape: pl.BlockSpec(shape, lambda i: (0,) * len(shape),
                                          pipeline_mode=pl.Buffered(1))
    return pl.pallas_call(
        functools.partial(_mixer_out_kernel, chunk_len=chunk_len),
        out_shape=[jax.ShapeDtypeStruct((m, d), F32), jax.ShapeDtypeStruct((m, d), BF16)],
        grid=(m // tm,),
        in_specs=[rows(d_in), rows(d_in), rows(d_in),
                  pl.BlockSpec((tm, d), lambda i: (i, 0)),
                  pl.BlockSpec((tm, d), lambda i: (i, 1)),
                  rows(d),
                  resident((ng, CHUNK, CHUNK)), resident((ng, CHUNK, 1)),
                  resident((d_in, d)), resident((d_in, d)), resident((d, d)), resident((1, d))],
        out_specs=[rows(d), rows(d)],
        scratch_shapes=[pltpu.VMEM((tm, d_in), BF16)],
        compiler_params=_params(("arbitrary",)),
        name="mixer_out",
    )(att, u, vg, gates, gates, x, wsp, bsp, wa, wb, wo, g)


def _ffn_kernel(h_ref, y_ref, w1_ref, w2_ref, g_ref, o_ref):
    f = pl.program_id(1)

    @pl.when(f == 0)
    def _init():
        o_ref[...] = y_ref[...]

    h = h_ref[...]
    half = w1_ref.shape[1] // 2
    a = [jnp.dot(h, w1_ref[:, c * half:(c + 1) * half], preferred_element_type=F32) for c in range(2)]
    part = None
    for c in range(2):
        r = jnp.square(jnp.maximum(a[c], 0.0)).astype(BF16)
        d = jnp.dot(r, w2_ref[c * half:(c + 1) * half, :], preferred_element_type=F32)
        part = d if part is None else part + d
    o_ref[...] += part

    @pl.when(f == pl.num_programs(1) - 1)
    def _final_norm():
        o_ref[...] = _rms(o_ref[...], g_ref[...])


def _ffn(h, y, w1, w2, g, tm, tf=2048):
    m, d = h.shape
    d_ff = w1.shape[1]
    return pl.pallas_call(
        _ffn_kernel,
        out_shape=jax.ShapeDtypeStruct((m, d), F32),
        grid=(m // tm, d_ff // tf),
        in_specs=[pl.BlockSpec((tm, d), lambda i, f: (i, 0)),
                  pl.BlockSpec((tm, d), lambda i, f: (i, 0)),
                  pl.BlockSpec((d, tf), lambda i, f: (0, f)),
                  pl.BlockSpec((tf, d), lambda i, f: (f, 0)),
                  pl.BlockSpec((1, d), lambda i, f: (0, 0))],
        out_specs=pl.BlockSpec((tm, d), lambda i, f: (i, 0)),
        compiler_params=_params(("arbitrary", "arbitrary"), V7X_VMEM_BUDGET_FFN),
        name="ffn",
    )(h, y, w1, w2, g)


def _token_path_in(x, norm_g, w_in, ln_g, ln_b, d_attn, d_gmlp, tm, vg_dtype, v_kind):
    m, d_model = x.shape
    c = 0
    q, h = _inproj(x, w_in, c, d_attn, "q", [BF16, BF16], _row_tile(m, TM_QNORM), extra=(norm_g,),
                   scale=HEAD_DIM ** -0.5 * LOG2_E)
    c += d_attn
    k32, k16 = _inproj(h, w_in, c, d_attn, "kv", [F32, BF16], tm)
    c += d_attn
    v32, v16 = _inproj(h, w_in, c, d_attn, v_kind, [F32, BF16], tm)
    c += d_attn
    (u,) = _inproj(h, w_in, c, d_gmlp, "u", [BF16], tm)
    c += d_gmlp
    (vg,) = _inproj(h, w_in, c, d_gmlp, "vs", [vg_dtype], tm, extra=(ln_g, ln_b))
    c += d_gmlp
    (gates,) = _inproj(h, w_in, c, 2 * d_model, "gates", [BF16], tm)
    return q, k32, k16, v32, v16, u, vg, gates


def _token_path_out(x, att, u, vg, gates, wsp, bsp, chunk_len, wa, wb, wo, norm_ffn_g, w1, w2,
                    norm_final_g, tm_mix, tm_ffn):
    y, h2 = _mixer_out(att, u, vg, gates, x, wsp, bsp, wa, wb, wo, norm_ffn_g, chunk_len, tm_mix)
    return _ffn(h2, y, w1, w2, norm_final_g, tm_ffn)


def kernel(x_prompt, x_sample, cache_k, cache_v, page_table, norm_mix_g, w_in, lambda_q1, lambda_k1,
           lambda_q2, lambda_k2, subln_g, gmlp_ln_g, gmlp_ln_b, w_spatial, b_spatial, w_branch_a,
           w_branch_b, w_out, norm_ffn_g, w_ff1, w_ff2, norm_final_g):
    bp, sp, d_model = x_prompt.shape
    bs, ts, _ = x_sample.shape
    depth = w_in.shape[0]
    assert bp == 1 and depth == 1, "the final norm is fused into the MLP kernel of the single layer"
    n_heads = cache_k.shape[3]
    d_attn = n_heads * V_DIM
    d_gmlp = gmlp_ln_g.shape[1]
    n_groups = w_spatial.shape[1]
    page = cache_k.shape[2]
    assert sp % CHUNK == 0 and CHUNK % ts == 0 and (bs * ts) % CHUNK == 0

    l = 0
    lam_init = 0.8 - 0.6 * math.exp(-0.3 * l)
    row2 = lambda a: a.reshape(1, -1)
    lams = [row2(lambda_q1[l]), row2(lambda_k1[l]), row2(lambda_q2[l]), row2(lambda_k2[l])]
    sub_g = row2(subln_g[l])
    ln_g, ln_b = row2(gmlp_ln_g[l]), row2(gmlp_ln_b[l])
    mix_g, ffn_g, fin_g = row2(norm_mix_g[l]), row2(norm_ffn_g[l]), row2(norm_final_g)

    xp = x_prompt.reshape(sp, d_model)
    q, k32, k16, v32, v16t, u, vg, gates = _token_path_in(
        xp, mix_g, w_in[l], ln_g, ln_b, d_attn, d_gmlp, _row_tile(sp, TM_INPROJ), BF16, "vt")
    att, (wa, wb, wo, w1, w2) = _attn_prompt(
        q, k16, v16t, lams, subln_g[l].reshape(-1, 1), lam_init, _row_tile(sp, ATTN_TQ),
        [w_branch_a[l], w_branch_b[l], w_out[l], w_ff1[l], w_ff2[l]])
    bsp_p = b_spatial[l].reshape(n_groups, CHUNK, 1)
    yp = _token_path_out(xp, att, u, vg, gates, w_spatial[l], bsp_p, CHUNK, wa, wb, wo, ffn_g,
                         w1, w2, fin_g, _row_tile(sp, TM_MIXER), _row_tile(sp, TM_FFN))
    new_k_prompt = k32.reshape(1, bp, sp, n_heads, V_DIM)
    new_v_prompt = v32.reshape(1, bp, sp, n_heads, V_DIM)

    ms = bs * ts
    xs = x_sample.reshape(ms, d_model)
    q, k32, k16, v32, v16, u, vg, gates = _token_path_in(
        xs, mix_g, w_in[l], ln_g, ln_b, d_attn, d_gmlp, _row_tile(ms, TM_INPROJ), F32, "kv")
    q5 = q.reshape(bs, ts, n_heads, 2, HEAD_DIM).transpose(0, 2, 3, 1, 4)
    eye = jnp.eye(2, dtype=BF16)
    q_rows = (q5[:, :, :, :, None, :] * eye[None, None, :, None, :, None]).reshape(
        bs, n_heads * 2 * ts, V_DIM)
    k_new = k16.reshape(bs, ts * n_heads, V_DIM)
    v_new = v16.reshape(bs, ts * n_heads, V_DIM)
    ck = cache_k[l].reshape(-1, page * n_heads, V_DIM)
    cv = cache_v[l].reshape(-1, page * n_heads, V_DIM)
    n_pages = page_table.shape[1]
    pages = math.gcd(n_pages, DECODE_PAGES)
    att_s = _attn_decode(q_rows, k_new, v_new, ck, cv, page_table, lams, sub_g, lam_init, ts,
                         n_heads, pages)
    att_s = att_s.reshape(ms, d_attn)
    reps = CHUNK // ts
    wsp_s = jnp.tile(w_spatial[l][:, :ts, :ts], (1, reps, reps))
    bsp_s = jnp.tile(b_spatial[l][:, :ts], (1, reps)).reshape(n_groups, CHUNK, 1)
    ys = _token_path_out(xs, att_s, u, vg, gates, wsp_s, bsp_s, ts, wa, wb, wo, ffn_g,
                         w1, w2, fin_g, _row_tile(ms, TM_MIXER), _row_tile(ms, TM_FFN))

    return (yp.reshape(bp, sp, d_model), ys.reshape(bs, ts, d_model), new_k_prompt, new_v_prompt,
            k32.reshape(1, bs, ts, n_heads, V_DIM), v32.reshape(1, bs, ts, n_heads, V_DIM),
            vg.reshape(1, bs, ts, d_gmlp))
```

```python
import functools
import math

import jax
import jax.numpy as jnp
from jax import lax
from jax.experimental import pallas as pl
from jax.experimental.pallas import tpu as pltpu

F32 = jnp.float32
BF16 = jnp.bfloat16

HEAD_DIM = 64
V_DIM = 2 * HEAD_DIM
CHUNK = 128
NEG_INF = -1e30
LOG2_E = math.log2(math.e)
RMS_EPS = 1e-6
LN_EPS = 1e-5
MIB = 1024 * 1024
V7X_VMEM_BUDGET = 56 * MIB
V7X_VMEM_BUDGET_LARGE = 60 * MIB
TM_QNORM = 512
TM_INPROJ = 1024
TM_MIXER = 256
TM_FFN = 512


def _row_tile(m, target):
    t = min(m, target)
    assert m % t == 0
    return t


def _params(sem, vmem=V7X_VMEM_BUDGET):
    return pltpu.CompilerParams(dimension_semantics=sem, vmem_limit_bytes=vmem)


def _rms(x, g):
    return x * lax.rsqrt(jnp.mean(x * x, axis=-1, keepdims=True) + RMS_EPS) * g


def _lambda(lq1, lk1, lq2, lk2, lam_init):
    a = jnp.exp(jnp.sum(lq1[...] * lk1[...], axis=1, keepdims=True))
    b = jnp.exp(jnp.sum(lq2[...] * lk2[...], axis=1, keepdims=True))
    return a - b + lam_init


def _inproj_kernel(h_ref, w_ref, *refs, kind, scale):
    wbf_ref = refs[-1]

    @pl.when(pl.program_id(1) == 0)
    def _cast_weights():
        wbf_ref[...] = w_ref[...].astype(BF16)

    if kind == "q":
        h = _rms(h_ref[...], refs[0][...]).astype(BF16)
        refs[2][...] = h
        z = jnp.dot(h, wbf_ref[...], preferred_element_type=F32)
        refs[1][...] = (z * scale).astype(BF16)
        return
    z = jnp.dot(h_ref[...], wbf_ref[...], preferred_element_type=F32)
    if kind == "kv":
        refs[0][...] = z
        refs[1][...] = z.astype(BF16)
    elif kind == "vt":
        refs[0][...] = z
        tk = refs[1].shape[2]
        for c in range(refs[1].shape[0]):
            refs[1][c] = z[c * tk:(c + 1) * tk].T.astype(BF16)
    elif kind == "u":
        refs[0][...] = jax.nn.gelu(z).astype(refs[0].dtype)
    elif kind == "vs":
        g_ref, b_ref, o_ref = refs[0], refs[1], refs[2]
        y = jax.nn.gelu(z)
        mu = jnp.mean(y, axis=-1, keepdims=True)
        yc = y - mu
        var = jnp.mean(yc * yc, axis=-1, keepdims=True)
        o_ref[...] = (yc * lax.rsqrt(var + LN_EPS) * g_ref[...] + b_ref[...]).astype(o_ref.dtype)
    elif kind == "gates":
        refs[0][...] = jax.nn.sigmoid(z).astype(refs[0].dtype)
    else:
        raise ValueError(kind)


def _inproj(h, w, col0, ncols, kind, out_dtypes, tm, extra=(), scale=1.0, tn=1024):
    m, d = h.shape
    assert col0 % tn == 0 and ncols % tn == 0
    j0 = col0 // tn
    in_specs = [pl.BlockSpec((tm, d), lambda j, i: (i, 0)),
                pl.BlockSpec((d, tn), lambda j, i: (0, j0 + j))]
    in_specs += [pl.BlockSpec((1, e.shape[1]), lambda j, i: (0, 0)) for e in extra]
    out_shape = [jax.ShapeDtypeStruct((m, ncols), dt) for dt in out_dtypes]
    out_specs = [pl.BlockSpec((tm, tn), lambda j, i: (i, j)) for _ in out_dtypes]
    if kind == "q":
        assert ncols == tn
        out_shape[1] = jax.ShapeDtypeStruct((m, d), out_dtypes[1])
        out_specs[1] = pl.BlockSpec((tm, d), lambda j, i: (i, 0))
    if kind == "vt":
        tk = min(tm, ATTN_TK)
        out_shape[1] = jax.ShapeDtypeStruct((m // tk, ncols, tk), out_dtypes[1])
        out_specs[1] = pl.BlockSpec((tm // tk, tn, tk), lambda j, i: (i, j, 0))
    outs = pl.pallas_call(
        functools.partial(_inproj_kernel, kind=kind, scale=scale),
        out_shape=out_shape,
        grid=(ncols // tn, m // tm),
        in_specs=in_specs,
        out_specs=out_specs,
        scratch_shapes=[pltpu.VMEM((d, tn), BF16)],
        compiler_params=_params(("arbitrary", "arbitrary")),
        name="inproj_" + kind,
    )(h, w, *extra)
    return outs


ATTN_TK = 512
ATTN_TQ = 2048
L_ROWS = 16
LANE_GROUP = 512


def _attn_prompt_kernel(lq1, lk1, lq2, lk2, sg_ref, q_ref, k_ref, vt_ref, *rest, tq, tk, lam_init,
                        n_cast):
    cast_in = rest[:n_cast]
    o_ref = rest[n_cast]
    cast_out = rest[n_cast + 1:2 * n_cast + 1]
    m_ref, acc_ref, s_ref = rest[2 * n_cast + 1:]
    for src, dst in zip(cast_in, cast_out):
        dst[...] = src[...].astype(dst.dtype)

    qi = pl.program_id(1)
    q = q_ref[...]
    lane = lax.broadcasted_iota(jnp.int32, q.shape, 1)
    zero = jnp.zeros_like(q)
    qb = jnp.concatenate([jnp.where(lane < HEAD_DIM, q, zero),
                          jnp.where(lane >= HEAD_DIM, q, zero)], axis=0)
    qbt = qb.astype(F32).T.astype(BF16)
    ones = jnp.ones((L_ROWS, tk), BF16)

    m_ref[...] = jnp.full(m_ref.shape, NEG_INF, F32)
    acc_ref[...] = jnp.zeros(acc_ref.shape, F32)

    n_groups = 2 * tq // LANE_GROUP
    lanes = [slice(g * LANE_GROUP, (g + 1) * LANE_GROUP) for g in range(n_groups)]
    n_full = qi * (tq // tk)

    def scores(ki, g):
        k = k_ref[pl.ds(pl.multiple_of(ki * tk, tk), tk), :]
        return jnp.dot(k, qbt[:, lanes[g]], preferred_element_type=F32)

    def first_query(g):
        return (g * LANE_GROUP) % tq

    def causal(s, g, j):
        key = lax.broadcasted_iota(jnp.int32, s.shape, 0) + j * tk
        qry = lax.broadcasted_iota(jnp.int32, s.shape, 1) + first_query(g)
        return jnp.where(key <= qry, s, NEG_INF)

    def values(ki):
        return jnp.concatenate([vt_ref[ki], ones], axis=0)

    def softmax_pv(s, g, vt):
        ls = lanes[g]
        m_old = m_ref[:, ls]
        m_new = jnp.maximum(m_old, jnp.max(s, axis=0, keepdims=True))
        alpha = jnp.exp2(m_old - m_new)
        p = jnp.exp2(s - m_new).astype(BF16)
        acc_ref[:, ls] = alpha * acc_ref[:, ls] + jnp.dot(vt, p, preferred_element_type=F32)
        m_ref[:, ls] = m_new

    s_ref[...] = scores(0, 0)

    def body(ki, carry):
        vt = values(ki)
        s = s_ref[...]
        for g in range(n_groups):
            if g + 1 < n_groups:
                s_next = scores(ki, g + 1)
            else:
                s_ref[...] = scores(ki + 1, 0)
            softmax_pv(s, g, vt)
            s = s_next
        return carry

    lax.fori_loop(0, n_full, body, 0)

    pairs = [(j, g) for j in range(tq // tk) for g in range(n_groups)
             if first_query(g) + LANE_GROUP - 1 >= j * tk]
    vts = [values(n_full + j) for j in range(tq // tk)]

    def masked_scores(j, g):
        s = s_ref[...] if (j == 0 and g == 0) else scores(n_full + j, g)
        fully_visible = first_query(g) >= (j + 1) * tk - 1
        return s if fully_visible else causal(s, g, j)

    s = masked_scores(*pairs[0])
    for idx, (j, g) in enumerate(pairs):
        if idx + 1 < len(pairs):
            s_next = masked_scores(*pairs[idx + 1])
        softmax_pv(s, g, vts[j])
        s = s_next

    acc = acc_ref[...]
    n = acc[:V_DIM] / acc[V_DIM:V_DIM + 1]
    lam = _lambda(lq1, lk1, lq2, lk2, lam_init)
    att = n[:, :tq] - lam * n[:, tq:]
    y = att * lax.rsqrt(jnp.mean(att * att, axis=0, keepdims=True) + RMS_EPS) * sg_ref[...]
    o_ref[...] = (y * (1.0 - lam_init)).T.astype(o_ref.dtype)


def _attn_prompt(q, k, vt, lams, subln_g_col, lam_init, tq, cast_weights):
    s, w = q.shape
    n_heads = w // V_DIM
    tk = vt.shape[2]
    n_q = s // tq
    n_steps = n_heads * n_q
    assert vt.shape == (s // tk, w, tk) and tq % tk == 0 and s % tq == 0
    assert (2 * tq) % LANE_GROUP == 0 and 2 * tq >= 2 * LANE_GROUP and tq % LANE_GROUP == 0
    bf16_sublanes = 16
    assert all(cw.shape[0] % (n_steps * bf16_sublanes) == 0 for cw in cast_weights)
    lam_specs = [pl.BlockSpec((1, HEAD_DIM), lambda h, i: (0, 0)) for _ in lams]
    slab_specs = [pl.BlockSpec((cw.shape[0] // n_steps, cw.shape[1]), lambda h, i: (h * n_q + i, 0))
                  for cw in cast_weights]
    outs = pl.pallas_call(
        functools.partial(_attn_prompt_kernel, tq=tq, tk=tk, lam_init=lam_init,
                          n_cast=len(cast_weights)),
        out_shape=[jax.ShapeDtypeStruct((s, w), BF16)]
        + [jax.ShapeDtypeStruct(cw.shape, BF16) for cw in cast_weights],
        grid=(n_heads, n_q),
        in_specs=lam_specs + [
            pl.BlockSpec((V_DIM, 1), lambda h, i: (0, 0)),
            pl.BlockSpec((tq, V_DIM), lambda h, i: (i, h)),
            pl.BlockSpec((s, V_DIM), lambda h, i: (0, h)),
            pl.BlockSpec((s // tk, V_DIM, tk), lambda h, i: (0, h, 0)),
        ] + slab_specs,
        out_specs=[pl.BlockSpec((tq, V_DIM), lambda h, i: (i, h))] + slab_specs,
        scratch_shapes=[pltpu.VMEM((1, 2 * tq), F32), pltpu.VMEM((V_DIM + L_ROWS, 2 * tq), F32),
                        pltpu.VMEM((tk, LANE_GROUP), F32)],
        compiler_params=_params(("arbitrary", "arbitrary")),
        name="attn_prompt",
    )(*lams, subln_g_col, q, k, vt, *cast_weights)
    return outs[0], outs[1:]


DECODE_PAGES = 16
DECODE_SUB_PAGES = 4
DECODE_RING = 3

def _attn_decode_kernel(pt_ref, lq1, lk1, lq2, lk2, sg_ref, q_ref, kn_ref, vn_ref, ck_hbm, cv_hbm,
                        o_ref, m_ref, l_ref, acc_ref, kbuf, vbuf, sem,
                        *, pages, sub, t_new, n_heads, lam_init):
    n_buf = kbuf.shape[0]
    page_rows = kbuf.shape[1] // pages
    t = pl.program_id(1)
    steps_per_seq = pl.num_programs(1)
    g = pl.program_id(0) * steps_per_seq + t
    n_steps = pl.num_programs(0) * steps_per_seq

    def step_copies(step):
        slot = lax.rem(step, n_buf)
        out = []
        for i in range(pages):
            pg = pt_ref[step * pages + i]
            rows = pl.ds(i * page_rows, page_rows)
            out.append(pltpu.make_async_copy(ck_hbm.at[pg], kbuf.at[slot, rows], sem.at[0, slot]))
            out.append(pltpu.make_async_copy(cv_hbm.at[pg], vbuf.at[slot, rows], sem.at[1, slot]))
        return out

    def start_step(step):
        for n, c in enumerate(step_copies(step)):
            c.start(priority=n % 2)

    @pl.when(g == 0)
    def _prime():
        for step in range(n_buf - 1):
            start_step(step)

    @pl.when(g + (n_buf - 1) < n_steps)
    def _prefetch():
        start_step(g + (n_buf - 1))

    for c in step_copies(g):
        c.wait()
    slot = lax.rem(g, n_buf)

    q = q_ref[0]
    rows_per_head = 2 * t_new
    nt = (((1,), (1,)), ((), ()))

    def own_head(shape):
        row = lax.broadcasted_iota(jnp.int32, shape, 0)
        col = lax.broadcasted_iota(jnp.int32, shape, 1)
        return (col % n_heads) == (row // rows_per_head), row, col

    @pl.when(t == 0)
    def _init_from_new_tokens():
        s = lax.dot_general(q, kn_ref[0], nt, preferred_element_type=F32)
        same, row, col = own_head(s.shape)
        s = jnp.where(same & ((col // n_heads) <= (row % t_new)), s, NEG_INF)
        m = jnp.max(s, axis=1, keepdims=True)
        p = jnp.exp2(s - m)
        m_ref[...] = m
        l_ref[...] = jnp.sum(p, axis=1, keepdims=True)
        acc_ref[...] = jnp.dot(p.astype(BF16), vn_ref[0], preferred_element_type=F32)

    def page(buf, i):
        return buf[slot, pl.ds(i * page_rows, page_rows), :].astype(BF16)

    def scores(blk):
        s = jnp.concatenate(
            [lax.dot_general(q, page(kbuf, blk * sub + i), nt, preferred_element_type=F32)
             for i in range(sub)], axis=1)
        same, _, _ = own_head(s.shape)
        return jnp.where(same, s, NEG_INF)

    def softmax_pv(s, blk):
        m_old = m_ref[...]
        m_new = jnp.maximum(m_old, jnp.max(s, axis=1, keepdims=True))
        alpha = jnp.exp2(m_old - m_new)
        p = jnp.exp2(s - m_new)
        l_ref[...] = alpha * l_ref[...] + jnp.sum(p, axis=1, keepdims=True)
        pb = p.astype(BF16)
        pv = None
        for i in range(sub):
            d = jnp.dot(pb[:, i * page_rows:(i + 1) * page_rows], page(vbuf, blk * sub + i),
                        preferred_element_type=F32)
            pv = d if pv is None else pv + d
        acc_ref[...] = alpha * acc_ref[...] + pv
        m_ref[...] = m_new

    n_blocks = pages // sub
    s = scores(0)
    for blk in range(n_blocks):
        if blk + 1 < n_blocks:
            s_next = scores(blk + 1)
        softmax_pv(s, blk)
        s = s_next

    @pl.when(t == pl.num_programs(1) - 1)
    def _finalize():
        lam = _lambda(lq1, lk1, lq2, lk2, lam_init)
        n = acc_ref[...] / l_ref[...]
        for h in range(n_heads):
            r0 = h * rows_per_head
            att = n[r0:r0 + t_new] - lam * n[r0 + t_new:r0 + rows_per_head]
            o_ref[0, :, h * V_DIM:(h + 1) * V_DIM] = _rms(att, sg_ref[...]) * (1.0 - lam_init)


def _attn_decode(q, k_new, v_new, cache_k, cache_v, page_table, lams, subln_g, lam_init, t_new,
                 n_heads, pages):
    bs, r, _ = q.shape
    n_pages = page_table.shape[1]
    sub = math.gcd(pages, DECODE_SUB_PAGES)
    assert n_pages % pages == 0 and bs * (n_pages // pages) >= DECODE_RING - 1
    pt_flat = page_table.reshape(-1)
    page_rows = cache_k.shape[1]

    const = lambda b, t, pt: (0, 0)
    per_seq = lambda b, t, pt: (b, 0, 0)
    grid_spec = pltpu.PrefetchScalarGridSpec(
        num_scalar_prefetch=1,
        grid=(bs, n_pages // pages),
        in_specs=[pl.BlockSpec((1, HEAD_DIM), const) for _ in lams] + [
            pl.BlockSpec((1, V_DIM), const),
            pl.BlockSpec((1, r, V_DIM), per_seq),
            pl.BlockSpec((1,) + k_new.shape[1:], per_seq),
            pl.BlockSpec((1,) + v_new.shape[1:], per_seq),
            pl.BlockSpec(memory_space=pl.ANY),
            pl.BlockSpec(memory_space=pl.ANY),
        ],
        out_specs=pl.BlockSpec((1, t_new, n_heads * V_DIM), per_seq),
        scratch_shapes=[pltpu.VMEM((r, 1), F32), pltpu.VMEM((r, 1), F32), pltpu.VMEM((r, V_DIM), F32),
                        pltpu.VMEM((DECODE_RING, pages * page_rows, V_DIM), F32),
                        pltpu.VMEM((DECODE_RING, pages * page_rows, V_DIM), F32),
                        pltpu.SemaphoreType.DMA((2, DECODE_RING))],
    )
    return pl.pallas_call(
        functools.partial(_attn_decode_kernel, pages=pages, sub=sub, t_new=t_new, n_heads=n_heads,
                          lam_init=lam_init),
        out_shape=jax.ShapeDtypeStruct((bs, t_new, n_heads * V_DIM), F32),
        grid_spec=grid_spec,
        compiler_params=_params(("arbitrary", "arbitrary"), V7X_VMEM_BUDGET_LARGE),
        name="attn_decode",
    )(pt_flat, *lams, subln_g, q, k_new, v_new, cache_k, cache_v)


def _mixer_out_kernel(att_ref, u_ref, vg_ref, ga_ref, gb_ref, x_ref, wsp_ref, bsp_ref, wa_ref, wb_ref,
                      wo_ref, g_ref, y_ref, h_ref, ug_ref, *, chunk_len):
    tm = u_ref.shape[0]
    n_groups = wsp_ref.shape[0]
    row = lax.broadcasted_iota(jnp.int32, (CHUNK, CHUNK), 0)
    col = lax.broadcasted_iota(jnp.int32, (CHUNK, CHUNK), 1)
    mask = (row // chunk_len == col // chunk_len) & (col <= row)
    for g in range(n_groups):
        wg = jnp.where(mask, wsp_ref[g], 0.0).astype(BF16)
        bg = bsp_ref[g]
        cs = slice(g * CHUNK, (g + 1) * CHUNK)
        for r in range(tm // CHUNK):
            rs = slice(r * CHUNK, (r + 1) * CHUNK)
            sg = jnp.dot(wg, vg_ref[rs, cs].astype(BF16), preferred_element_type=F32) + bg
            ug_ref[rs, cs] = (u_ref[rs, cs].astype(F32) * sg).astype(BF16)
    a = jnp.dot(att_ref[...].astype(BF16), wa_ref[...], preferred_element_type=F32)
    gp = jnp.dot(ug_ref[...], wb_ref[...], preferred_element_type=F32)
    mm = (ga_ref[...].astype(F32) * a + gb_ref[...].astype(F32) * gp).astype(BF16)
    y = x_ref[...] + jnp.dot(mm, wo_ref[...], preferred_element_type=F32)
    y_ref[...] = y
    h_ref[...] = _rms(y, g_ref[...]).astype(h_ref.dtype)


def _mixer_out(att, u, vg, gates, x, wsp, bsp, wa, wb, wo, g, chunk_len, tm):
    m, d = x.shape
    d_in = att.shape[1]
    ng = wsp.shape[0]
    rows = lambda w: pl.BlockSpec((tm, w), lambda i: (i, 0))
    resident = lambda shape: pl.BlockSpec(shape, lambda i: (0,) * len(shape),
                                          pipeline_mode=pl.Buffered(1))
    return pl.pallas_call(
        functools.partial(_mixer_out_kernel, chunk_len=chunk_len),
        out_shape=[jax.ShapeDtypeStruct((m, d), F32), jax.ShapeDtypeStruct((m, d), BF16)],
        grid=(m // tm,),
        in_specs=[rows(d_in), rows(d_in), rows(d_in),
                  pl.BlockSpec((tm, d), lambda i: (i, 0)),
                  pl.BlockSpec((tm, d), lambda i: (i, 1)),
                  rows(d),
                  resident((ng, CHUNK, CHUNK)), resident((ng, CHUNK, 1)),
                  resident((d_in, d)), resident((d_in, d)), resident((d, d)), resident((1, d))],
        out_specs=[rows(d), rows(d)],
        scratch_shapes=[pltpu.VMEM((tm, d_in), BF16)],
        compiler_params=_params(("arbitrary",)),
        name="mixer_out",
    )(att, u, vg, gates, gates, x, wsp, bsp, wa, wb, wo, g)


def _ffn_kernel(h_ref, y_ref, w1_ref, w2_ref, g_ref, o_ref):
    f = pl.program_id(1)

    @pl.when(f == 0)
    def _init():
        o_ref[...] = y_ref[...]

    h = h_ref[...]
    half = w1_ref.shape[1] // 2
    a = [jnp.dot(h, w1_ref[:, c * half:(c + 1) * half], preferred_element_type=F32) for c in range(2)]
    part = None
    for c in range(2):
        r = jnp.square(jnp.maximum(a[c], 0.0)).astype(BF16)
        d = jnp.dot(r, w2_ref[c * half:(c + 1) * half, :], preferred_element_type=F32)
        part = d if part is None else part + d
    o_ref[...] += part

    @pl.when(f == pl.num_programs(1) - 1)
    def _final_norm():
        o_ref[...] = _rms(o_ref[...], g_ref[...])


def _ffn(h, y, w1, w2, g, tm, tf=2048):
    m, d = h.shape
    d_ff = w1.shape[1]
    return pl.pallas_call(
        _ffn_kernel,
        out_shape=jax.ShapeDtypeStruct((m, d), F32),
        grid=(m // tm, d_ff // tf),
        in_specs=[pl.BlockSpec((tm, d), lambda i, f: (i, 0)),
                  pl.BlockSpec((tm, d), lambda i, f: (i, 0)),
                  pl.BlockSpec((d, tf), lambda i, f: (0, f)),
                  pl.BlockSpec((tf, d), lambda i, f: (f, 0)),
                  pl.BlockSpec((1, d), lambda i, f: (0, 0))],
        out_specs=pl.BlockSpec((tm, d), lambda i, f: (i, 0)),
        compiler_params=_params(("arbitrary", "arbitrary"), V7X_VMEM_BUDGET_LARGE),
        name="ffn",
    )(h, y, w1, w2, g)


def _token_path_in(x, norm_g, w_in, ln_g, ln_b, d_attn, d_gmlp, tm, vg_dtype, v_kind):
    m, d_model = x.shape
    c = 0
    q, h = _inproj(x, w_in, c, d_attn, "q", [BF16, BF16], _row_tile(m, TM_QNORM), extra=(norm_g,),
                   scale=HEAD_DIM ** -0.5 * LOG2_E)
    c += d_attn
    k32, k16 = _inproj(h, w_in, c, d_attn, "kv", [F32, BF16], tm)
    c += d_attn
    v32, v16 = _inproj(h, w_in, c, d_attn, v_kind, [F32, BF16], tm)
    c += d_attn
    (u,) = _inproj(h, w_in, c, d_gmlp, "u", [BF16], tm)
    c += d_gmlp
    (vg,) = _inproj(h, w_in, c, d_gmlp, "vs", [vg_dtype], tm, extra=(ln_g, ln_b))
    c += d_gmlp
    (gates,) = _inproj(h, w_in, c, 2 * d_model, "gates", [BF16], tm)
    return q, k32, k16, v32, v16, u, vg, gates


def _token_path_out(x, att, u, vg, gates, wsp, bsp, chunk_len, wa, wb, wo, norm_ffn_g, w1, w2,
                    norm_final_g, tm_mix, tm_ffn):
    y, h2 = _mixer_out(att, u, vg, gates, x, wsp, bsp, wa, wb, wo, norm_ffn_g, chunk_len, tm_mix)
    return _ffn(h2, y, w1, w2, norm_final_g, tm_ffn)


def kernel(x_prompt, x_sample, cache_k, cache_v, page_table, norm_mix_g, w_in, lambda_q1, lambda_k1,
           lambda_q2, lambda_k2, subln_g, gmlp_ln_g, gmlp_ln_b, w_spatial, b_spatial, w_branch_a,
           w_branch_b, w_out, norm_ffn_g, w_ff1, w_ff2, norm_final_g):
    bp, sp, d_model = x_prompt.shape
    bs, ts, _ = x_sample.shape
    depth = w_in.shape[0]
    assert bp == 1 and depth == 1, "the final norm is fused into the MLP kernel of the single layer"
    n_heads = cache_k.shape[3]
    d_attn = n_heads * V_DIM
    d_gmlp = gmlp_ln_g.shape[1]
    n_groups = w_spatial.shape[1]
    page = cache_k.shape[2]
    assert sp % CHUNK == 0 and CHUNK % ts == 0 and (bs * ts) % CHUNK == 0

    l = 0
    lam_init = 0.8 - 0.6 * math.exp(-0.3 * l)
    row2 = lambda a: a.reshape(1, -1)
    lams = [row2(lambda_q1[l]), row2(lambda_k1[l]), row2(lambda_q2[l]), row2(lambda_k2[l])]
    sub_g = row2(subln_g[l])
    ln_g, ln_b = row2(gmlp_ln_g[l]), row2(gmlp_ln_b[l])
    mix_g, ffn_g, fin_g = row2(norm_mix_g[l]), row2(norm_ffn_g[l]), row2(norm_final_g)

    xp = x_prompt.reshape(sp, d_model)
    q, k32, k16, v32, v16t, u, vg, gates = _token_path_in(
        xp, mix_g, w_in[l], ln_g, ln_b, d_attn, d_gmlp, _row_tile(sp, TM_INPROJ), BF16, "vt")
    att, (wa, wb, wo, w1, w2) = _attn_prompt(
        q, k16, v16t, lams, subln_g[l].reshape(-1, 1), lam_init, _row_tile(sp, ATTN_TQ),
        [w_branch_a[l], w_branch_b[l], w_out[l], w_ff1[l], w_ff2[l]])
    bsp_p = b_spatial[l].reshape(n_groups, CHUNK, 1)
    yp = _token_path_out(xp, att, u, vg, gates, w_spatial[l], bsp_p, CHUNK, wa, wb, wo, ffn_g,
                         w1, w2, fin_g, _row_tile(sp, TM_MIXER), _row_tile(sp, TM_FFN))
    new_k_prompt = k32.reshape(1, bp, sp, n_heads, V_DIM)
    new_v_prompt = v32.reshape(1, bp, sp, n_heads, V_DIM)

    ms = bs * ts
    xs = x_sample.reshape(ms, d_model)
    q, k32, k16, v32, v16, u, vg, gates = _token_path_in(
        xs, mix_g, w_in[l], ln_g, ln_b, d_attn, d_gmlp, _row_tile(ms, TM_INPROJ), F32, "kv")
    q5 = q.reshape(bs, ts, n_heads, 2, HEAD_DIM).transpose(0, 2, 3, 1, 4)
    eye = jnp.eye(2, dtype=BF16)
    q_rows = (q5[:, :, :, :, None, :] * eye[None, None, :, None, :, None]).reshape(
        bs, n_heads * 2 * ts, V_DIM)
    k_new = k16.reshape(bs, ts * n_heads, V_DIM)
    v_new = v16.reshape(bs, ts * n_heads, V_DIM)
    ck = cache_k[l].reshape(-1, page * n_heads, V_DIM)
    cv = cache_v[l].reshape(-1, page * n_heads, V_DIM)
    n_pages = page_table.shape[1]
    pages = math.gcd(n_pages, DECODE_PAGES)
    att_s = _attn_decode(q_rows, k_new, v_new, ck, cv, page_table, lams, sub_g, lam_init, ts,
                         n_heads, pages)
    att_s = att_s.reshape(ms, d_attn)
    reps = CHUNK // ts
    wsp_s = jnp.tile(w_spatial[l][:, :ts, :ts], (1, reps, reps))
    bsp_s = jnp.tile(b_spatial[l][:, :ts], (1, reps)).reshape(n_groups, CHUNK, 1)
    ys = _token_path_out(xs, att_s, u, vg, gates, wsp_s, bsp_s, ts, wa, wb, wo, ffn_g,
                         w1, w2, fin_g, _row_tile(ms, TM_MIXER), _row_tile(ms, TM_FFN))

    return (yp.reshape(bp, sp, d_model), ys.reshape(bs, ts, d_model), new_k_prompt, new_v_prompt,
            k32.reshape(1, bs, ts, n_heads, V_DIM), v32.reshape(1, bs, ts, n_heads, V_DIM),
            vg.reshape(1, bs, ts, d_gmlp))
```

```python
import functools
import math

import jax
import jax.numpy as jnp
from jax import lax
from jax.experimental import pallas as pl
from jax.experimental.pallas import tpu as pltpu

F32 = jnp.float32
BF16 = jnp.bfloat16

HEAD_DIM = 64
V_DIM = 2 * HEAD_DIM
CHUNK = 128
NEG_INF = -1e30
LOG2_E = math.log2(math.e)
RMS_EPS = 1e-6
LN_EPS = 1e-5
MIB = 1024 * 1024
V7X_VMEM_BUDGET = 56 * MIB
V7X_VMEM_BUDGET_FFN = 60 * MIB
TM_QNORM = 512
TM_INPROJ = 1024
TM_MIXER = 256
TM_FFN = 512


def _row_tile(m, target):
    t = min(m, target)
    assert m % t == 0
    return t


def _params(sem, vmem=V7X_VMEM_BUDGET):
    return pltpu.CompilerParams(dimension_semantics=sem, vmem_limit_bytes=vmem)


def _rms(x, g):
    return x * lax.rsqrt(jnp.mean(x * x, axis=-1, keepdims=True) + RMS_EPS) * g


def _lambda(lq1, lk1, lq2, lk2, lam_init):
    a = jnp.exp(jnp.sum(lq1[...] * lk1[...], axis=1, keepdims=True))
    b = jnp.exp(jnp.sum(lq2[...] * lk2[...], axis=1, keepdims=True))
    return a - b + lam_init


def _inproj_kernel(h_ref, w_ref, *refs, kind, scale):
    wbf_ref = refs[-1]

    @pl.when(pl.program_id(1) == 0)
    def _cast_weights():
        wbf_ref[...] = w_ref[...].astype(BF16)

    if kind == "q":
        h = _rms(h_ref[...], refs[0][...]).astype(BF16)
        refs[2][...] = h
        z = jnp.dot(h, wbf_ref[...], preferred_element_type=F32)
        refs[1][...] = (z * scale).astype(BF16)
        return
    z = jnp.dot(h_ref[...], wbf_ref[...], preferred_element_type=F32)
    if kind == "kv":
        refs[0][...] = z
        refs[1][...] = z.astype(BF16)
    elif kind == "vt":
        refs[0][...] = z
        tk = refs[1].shape[2]
        for c in range(refs[1].shape[0]):
            refs[1][c] = z[c * tk:(c + 1) * tk].T.astype(BF16)
    elif kind == "u":
        refs[0][...] = jax.nn.gelu(z).astype(refs[0].dtype)
    elif kind == "vs":
        g_ref, b_ref, o_ref = refs[0], refs[1], refs[2]
        y = jax.nn.gelu(z)
        mu = jnp.mean(y, axis=-1, keepdims=True)
        yc = y - mu
        var = jnp.mean(yc * yc, axis=-1, keepdims=True)
        o_ref[...] = (yc * lax.rsqrt(var + LN_EPS) * g_ref[...] + b_ref[...]).astype(o_ref.dtype)
    elif kind == "gates":
        refs[0][...] = jax.nn.sigmoid(z).astype(refs[0].dtype)
    else:
        raise ValueError(kind)


def _inproj(h, w, col0, ncols, kind, out_dtypes, tm, extra=(), scale=1.0, tn=1024):
    m, d = h.shape
    assert col0 % tn == 0 and ncols % tn == 0
    j0 = col0 // tn
    in_specs = [pl.BlockSpec((tm, d), lambda j, i: (i, 0)),
                pl.BlockSpec((d, tn), lambda j, i: (0, j0 + j))]
    in_specs += [pl.BlockSpec((1, e.shape[1]), lambda j, i: (0, 0)) for e in extra]
    out_shape = [jax.ShapeDtypeStruct((m, ncols), dt) for dt in out_dtypes]
    out_specs = [pl.BlockSpec((tm, tn), lambda j, i: (i, j)) for _ in out_dtypes]
    if kind == "q":
        assert ncols == tn
        out_shape[1] = jax.ShapeDtypeStruct((m, d), out_dtypes[1])
        out_specs[1] = pl.BlockSpec((tm, d), lambda j, i: (i, 0))
    if kind == "vt":
        tk = min(tm, ATTN_TK)
        out_shape[1] = jax.ShapeDtypeStruct((m // tk, ncols, tk), out_dtypes[1])
        out_specs[1] = pl.BlockSpec((tm // tk, tn, tk), lambda j, i: (i, j, 0))
    outs = pl.pallas_call(
        functools.partial(_inproj_kernel, kind=kind, scale=scale),
        out_shape=out_shape,
        grid=(ncols // tn, m // tm),
        in_specs=in_specs,
        out_specs=out_specs,
        scratch_shapes=[pltpu.VMEM((d, tn), BF16)],
        compiler_params=_params(("arbitrary", "arbitrary")),
        name="inproj_" + kind,
    )(h, w, *extra)
    return outs


ATTN_TK = 512
ATTN_TQ = 2048
L_ROWS = 16
LANE_GROUP = 512


def _attn_prompt_kernel(lq1, lk1, lq2, lk2, sg_ref, q_ref, k_ref, vt_ref, *rest, tq, tk, lam_init,
                        n_cast):
    cast_in = rest[:n_cast]
    o_ref = rest[n_cast]
    cast_out = rest[n_cast + 1:2 * n_cast + 1]
    m_ref, acc_ref, s_ref = rest[2 * n_cast + 1:]
    for src, dst in zip(cast_in, cast_out):
        dst[...] = src[...].astype(dst.dtype)

    qi = pl.program_id(1)
    q = q_ref[...]
    lane = lax.broadcasted_iota(jnp.int32, q.shape, 1)
    zero = jnp.zeros_like(q)
    qb = jnp.concatenate([jnp.where(lane < HEAD_DIM, q, zero),
                          jnp.where(lane >= HEAD_DIM, q, zero)], axis=0)
    qbt = qb.astype(F32).T.astype(BF16)
    ones = jnp.ones((L_ROWS, tk), BF16)

    m_ref[...] = jnp.full(m_ref.shape, NEG_INF, F32)
    acc_ref[...] = jnp.zeros(acc_ref.shape, F32)

    n_groups = 2 * tq // LANE_GROUP
    lanes = [slice(g * LANE_GROUP, (g + 1) * LANE_GROUP) for g in range(n_groups)]
    n_full = qi * (tq // tk)

    def scores(ki, g):
        k = k_ref[pl.ds(pl.multiple_of(ki * tk, tk), tk), :]
        return jnp.dot(k, qbt[:, lanes[g]], preferred_element_type=F32)

    def first_query(g):
        return (g * LANE_GROUP) % tq

    def causal(s, g, j):
        key = lax.broadcasted_iota(jnp.int32, s.shape, 0) + j * tk
        qry = lax.broadcasted_iota(jnp.int32, s.shape, 1) + first_query(g)
        return jnp.where(key <= qry, s, NEG_INF)

    def values(ki):
        return jnp.concatenate([vt_ref[ki], ones], axis=0)

    def softmax_pv(s, g, vt):
        ls = lanes[g]
        m_old = m_ref[:, ls]
        m_new = jnp.maximum(m_old, jnp.max(s, axis=0, keepdims=True))
        alpha = jnp.exp2(m_old - m_new)
        p = jnp.exp2(s - m_new).astype(BF16)
        acc_ref[:, ls] = alpha * acc_ref[:, ls] + jnp.dot(vt, p, preferred_element_type=F32)
        m_ref[:, ls] = m_new

    s_ref[...] = scores(0, 0)

    def body(ki, carry):
        vt = values(ki)
        s = s_ref[...]
        for g in range(n_groups):
            if g + 1 < n_groups:
                s_next = scores(ki, g + 1)
            else:
                s_ref[...] = scores(ki + 1, 0)
            softmax_pv(s, g, vt)
            s = s_next
        return carry

    lax.fori_loop(0, n_full, body, 0)

    pairs = [(j, g) for j in range(tq // tk) for g in range(n_groups)
             if first_query(g) + LANE_GROUP - 1 >= j * tk]
    vts = [values(n_full + j) for j in range(tq // tk)]

    def masked_scores(j, g):
        s = s_ref[...] if (j == 0 and g == 0) else scores(n_full + j, g)
        fully_visible = first_query(g) >= (j + 1) * tk - 1
        return s if fully_visible else causal(s, g, j)

    s = masked_scores(*pairs[0])
    for idx, (j, g) in enumerate(pairs):
        if idx + 1 < len(pairs):
            s_next = masked_scores(*pairs[idx + 1])
        softmax_pv(s, g, vts[j])
        s = s_next

    acc = acc_ref[...]
    n = acc[:V_DIM] / acc[V_DIM:V_DIM + 1]
    lam = _lambda(lq1, lk1, lq2, lk2, lam_init)
    att = n[:, :tq] - lam * n[:, tq:]
    y = att * lax.rsqrt(jnp.mean(att * att, axis=0, keepdims=True) + RMS_EPS) * sg_ref[...]
    o_ref[...] = (y * (1.0 - lam_init)).T.astype(o_ref.dtype)


def _attn_prompt(q, k, vt, lams, subln_g_col, lam_init, tq, cast_weights):
    s, w = q.shape
    n_heads = w // V_DIM
    tk = vt.shape[2]
    n_q = s // tq
    n_steps = n_heads * n_q
    assert vt.shape == (s // tk, w, tk) and tq % tk == 0 and s % tq == 0
    assert (2 * tq) % LANE_GROUP == 0 and 2 * tq >= 2 * LANE_GROUP and tq % LANE_GROUP == 0
    bf16_sublanes = 16
    assert all(cw.shape[0] % (n_steps * bf16_sublanes) == 0 for cw in cast_weights)
    lam_specs = [pl.BlockSpec((1, HEAD_DIM), lambda h, i: (0, 0)) for _ in lams]
    slab_specs = [pl.BlockSpec((cw.shape[0] // n_steps, cw.shape[1]), lambda h, i: (h * n_q + i, 0))
                  for cw in cast_weights]
    outs = pl.pallas_call(
        functools.partial(_attn_prompt_kernel, tq=tq, tk=tk, lam_init=lam_init,
                          n_cast=len(cast_weights)),
        out_shape=[jax.ShapeDtypeStruct((s, w), BF16)]
        + [jax.ShapeDtypeStruct(cw.shape, BF16) for cw in cast_weights],
        grid=(n_heads, n_q),
        in_specs=lam_specs + [
            pl.BlockSpec((V_DIM, 1), lambda h, i: (0, 0)),
            pl.BlockSpec((tq, V_DIM), lambda h, i: (i, h)),
            pl.BlockSpec((s, V_DIM), lambda h, i: (0, h)),
            pl.BlockSpec((s // tk, V_DIM, tk), lambda h, i: (0, h, 0)),
        ] + slab_specs,
        out_specs=[pl.BlockSpec((tq, V_DIM), lambda h, i: (i, h))] + slab_specs,
        scratch_shapes=[pltpu.VMEM((1, 2 * tq), F32), pltpu.VMEM((V_DIM + L_ROWS, 2 * tq), F32),
                        pltpu.VMEM((tk, LANE_GROUP), F32)],
        compiler_params=_params(("arbitrary", "arbitrary")),
        name="attn_prompt",
    )(*lams, subln_g_col, q, k, vt, *cast_weights)
    return outs[0], outs[1:]


DECODE_PAGES = 8
DECODE_SUB_PAGES = 4
DECODE_RING = 4

def _attn_decode_kernel(pt_ref, lq1, lk1, lq2, lk2, sg_ref, q_ref, kn_ref, vn_ref, ck_hbm, cv_hbm,
                        o_ref, m_ref, l_ref, acc_ref, kbuf, vbuf, sem,
                        *, pages, sub, t_new, n_heads, lam_init):
    n_buf = kbuf.shape[0]
    page_rows = kbuf.shape[1] // pages
    t = pl.program_id(1)
    steps_per_seq = pl.num_programs(1)
    g = pl.program_id(0) * steps_per_seq + t
    n_steps = pl.num_programs(0) * steps_per_seq

    def step_copies(step):
        slot = lax.rem(step, n_buf)
        out = []
        for i in range(pages):
            pg = pt_ref[step * pages + i]
            rows = pl.ds(i * page_rows, page_rows)
            out.append(pltpu.make_async_copy(ck_hbm.at[pg], kbuf.at[slot, rows], sem.at[0, slot]))
            out.append(pltpu.make_async_copy(cv_hbm.at[pg], vbuf.at[slot, rows], sem.at[1, slot]))
        return out

    def start_step(step):
        for n, c in enumerate(step_copies(step)):
            c.start(priority=n % 2)

    @pl.when(g == 0)
    def _prime():
        for step in range(n_buf - 1):
            start_step(step)

    @pl.when(g + (n_buf - 1) < n_steps)
    def _prefetch():
        start_step(g + (n_buf - 1))

    for c in step_copies(g):
        c.wait()
    slot = lax.rem(g, n_buf)

    q = q_ref[0]
    rows_per_head = 2 * t_new
    nt = (((1,), (1,)), ((), ()))

    def own_head(shape):
        row = lax.broadcasted_iota(jnp.int32, shape, 0)
        col = lax.broadcasted_iota(jnp.int32, shape, 1)
        return (col % n_heads) == (row // rows_per_head), row, col

    @pl.when(t == 0)
    def _init_from_new_tokens():
        s = lax.dot_general(q, kn_ref[0], nt, preferred_element_type=F32)
        same, row, col = own_head(s.shape)
        s = jnp.where(same & ((col // n_heads) <= (row % t_new)), s, NEG_INF)
        m = jnp.max(s, axis=1, keepdims=True)
        p = jnp.exp2(s - m)
        m_ref[...] = m
        l_ref[...] = jnp.sum(p, axis=1, keepdims=True)
        acc_ref[...] = jnp.dot(p.astype(BF16), vn_ref[0], preferred_element_type=F32)

    def page(buf, i):
        return buf[slot, pl.ds(i * page_rows, page_rows), :].astype(BF16)

    def scores(blk):
        s = jnp.concatenate(
            [lax.dot_general(q, page(kbuf, blk * sub + i), nt, preferred_element_type=F32)
             for i in range(sub)], axis=1)
        same, _, _ = own_head(s.shape)
        return jnp.where(same, s, NEG_INF)

    def softmax_pv(s, blk):
        m_old = m_ref[...]
        m_new = jnp.maximum(m_old, jnp.max(s, axis=1, keepdims=True))
        alpha = jnp.exp2(m_old - m_new)
        p = jnp.exp2(s - m_new)
        l_ref[...] = alpha * l_ref[...] + jnp.sum(p, axis=1, keepdims=True)
        pb = p.astype(BF16)
        pv = None
        for i in range(sub):
            d = jnp.dot(pb[:, i * page_rows:(i + 1) * page_rows], page(vbuf, blk * sub + i),
                        preferred_element_type=F32)
            pv = d if pv is None else pv + d
        acc_ref[...] = alpha * acc_ref[...] + pv
        m_ref[...] = m_new

    n_blocks = pages // sub
    s = scores(0)
    for blk in range(n_blocks):
        if blk + 1 < n_blocks:
            s_next = scores(blk + 1)
        softmax_pv(s, blk)
        s = s_next

    @pl.when(t == pl.num_programs(1) - 1)
    def _finalize():
        lam = _lambda(lq1, lk1, lq2, lk2, lam_init)
        n = acc_ref[...] / l_ref[...]
        for h in range(n_heads):
            r0 = h * rows_per_head
            att = n[r0:r0 + t_new] - lam * n[r0 + t_new:r0 + rows_per_head]
            o_ref[0, :, h * V_DIM:(h + 1) * V_DIM] = _rms(att, sg_ref[...]) * (1.0 - lam_init)


def _attn_decode(q, k_new, v_new, cache_k, cache_v, page_table, lams, subln_g, lam_init, t_new,
                 n_heads, pages):
    bs, r, _ = q.shape
    n_pages = page_table.shape[1]
    sub = math.gcd(pages, DECODE_SUB_PAGES)
    assert n_pages % pages == 0 and bs * (n_pages // pages) >= DECODE_RING - 1
    pt_flat = page_table.reshape(-1)
    page_rows = cache_k.shape[1]

    const = lambda b, t, pt: (0, 0)
    per_seq = lambda b, t, pt: (b, 0, 0)
    grid_spec = pltpu.PrefetchScalarGridSpec(
        num_scalar_prefetch=1,
        grid=(bs, n_pages // pages),
        in_specs=[pl.BlockSpec((1, HEAD_DIM), const) for _ in lams] + [
            pl.BlockSpec((1, V_DIM), const),
            pl.BlockSpec((1, r, V_DIM), per_seq),
            pl.BlockSpec((1,) + k_new.shape[1:], per_seq),
            pl.BlockSpec((1,) + v_new.shape[1:], per_seq),
            pl.BlockSpec(memory_space=pl.ANY),
            pl.BlockSpec(memory_space=pl.ANY),
        ],
        out_specs=pl.BlockSpec((1, t_new, n_heads * V_DIM), per_seq),
        scratch_shapes=[pltpu.VMEM((r, 1), F32), pltpu.VMEM((r, 1), F32), pltpu.VMEM((r, V_DIM), F32),
                        pltpu.VMEM((DECODE_RING, pages * page_rows, V_DIM), F32),
                        pltpu.VMEM((DECODE_RING, pages * page_rows, V_DIM), F32),
                        pltpu.SemaphoreType.DMA((2, DECODE_RING))],
    )
    return pl.pallas_call(
        functools.partial(_attn_decode_kernel, pages=pages, sub=sub, t_new=t_new, n_heads=n_heads,
                          lam_init=lam_init),
        out_shape=jax.ShapeDtypeStruct((bs, t_new, n_heads * V_DIM), F32),
        grid_spec=grid_spec,
        compiler_params=_params(("arbitrary", "arbitrary")),
        name="attn_decode",
    )(pt_flat, *lams, subln_g, q, k_new, v_new, cache_k, cache_v)


def _mixer_out_kernel(att_ref, u_ref, vg_ref, ga_ref, gb_ref, x_ref, wsp_ref, bsp_ref, wa_ref, wb_ref,
                      wo_ref, g_ref, y_ref, h_ref, ug_ref, *, chunk_len):
    tm = u_ref.shape[0]
    n_groups = wsp_ref.shape[0]
    row = lax.broadcasted_iota(jnp.int32, (CHUNK, CHUNK), 0)
    col = lax.broadcasted_iota(jnp.int32, (CHUNK, CHUNK), 1)
    mask = (row // chunk_len == col // chunk_len) & (col <= row)
    for g in range(n_groups):
        wg = jnp.where(mask, wsp_ref[g], 0.0).astype(BF16)
        bg = bsp_ref[g]
        cs = slice(g * CHUNK, (g + 1) * CHUNK)
        for r in range(tm // CHUNK):
            rs = slice(r * CHUNK, (r + 1) * CHUNK)
            sg = jnp.dot(wg, vg_ref[rs, cs].astype(BF16), preferred_element_type=F32) + bg
            ug_ref[rs, cs] = (u_ref[rs, cs].astype(F32) * sg).astype(BF16)
    a = jnp.dot(att_ref[...].astype(BF16), wa_ref[...], preferred_element_type=F32)
    gp = jnp.dot(ug_ref[...], wb_ref[...], preferred_element_type=F32)
    mm = (ga_ref[...].astype(F32) * a + gb_ref[...].astype(F32) * gp).astype(BF16)
    y = x_ref[...] + jnp.dot(mm, wo_ref[...], preferred_element_type=F32)
    y_ref[...] = y
    h_ref[...] = _rms(y, g_ref[...]).astype(h_ref.dtype)


def _mixer_out(att, u, vg, gates, x, wsp, bsp, wa, wb, wo, g, chunk_len, tm):
    m, d = x.shape
    d_in = att.shape[1]
    ng = wsp.shape[0]
    rows = lambda w: pl.BlockSpec((tm, w), lambda i: (i, 0))
    resident = lambda shape: pl.BlockSpec(shape, lambda i: (0,) * len(shape),
                                          pipeline_mode=pl.Buffered(1))
    return pl.pallas_call(
        functools.partial(_mixer_out_kernel, chunk_len=chunk_len),
        out_shape=[jax.ShapeDtypeStruct((m, d), F32), jax.ShapeDtypeStruct((m, d), BF16)],
        grid=(m // tm,),
        in_specs=[rows(d_in), rows(d_in), rows(d_in),
                  pl.BlockSpec((tm, d), lambda i: (i, 0)),
                  pl.BlockSpec((tm, d), lambda i: (i, 1)),
                  rows(d),
                  resident((ng, CHUNK, CHUNK)), resident((ng, CHUNK, 1)),
                  resident((d_in, d)), resident((d_in, d)), resident((d, d)), resident((1, d))],
        out_specs=[rows(d), rows(d)],
        scratch_shapes=[pltpu.VMEM((tm, d_in), BF16)],
        compiler_params=_params(("arbitrary",)),
        name="mixer_out",
    )(att, u, vg, gates, gates, x, wsp, bsp, wa, wb, wo, g)


def _ffn_kernel(h_ref, y_ref, w1_ref, w2_ref, g_ref, hs_ref, ys_ref, o_ref, os_ref):
    i = pl.program_id(0)
    f = pl.program_id(1)
    last = pl.num_programs(1) - 1
    half = w1_ref.shape[1] // 2

    def mlp_tile(h):
        a = [jnp.dot(h, w1_ref[:, c * half:(c + 1) * half], preferred_element_type=F32)
             for c in range(2)]
        part = None
        for c in range(2):
            r = jnp.square(jnp.maximum(a[c], 0.0)).astype(BF16)
            d = jnp.dot(r, w2_ref[c * half:(c + 1) * half, :], preferred_element_type=F32)
            part = d if part is None else part + d
        return part

    def accumulate(h_in, y_in, out):
        @pl.when(f == 0)
        def _init():
            out[...] = y_in[...]

        out[...] += mlp_tile(h_in[...])

        @pl.when(f == last)
        def _final_norm():
            out[...] = _rms(out[...], g_ref[...])

    accumulate(h_ref, y_ref, o_ref)

    @pl.when(i == 0)
    def _ride_along():
        accumulate(hs_ref, ys_ref, os_ref)


def _ffn(h, y, w1, w2, g, tm, hs, ys, tf=2048):
    m, d = h.shape
    d_ff = w1.shape[1]
    whole = lambda a, **kw: pl.BlockSpec(a.shape, lambda i, f: (0, 0), **kw)
    return pl.pallas_call(
        _ffn_kernel,
        out_shape=[jax.ShapeDtypeStruct((m, d), F32), jax.ShapeDtypeStruct(ys.shape, F32)],
        grid=(m // tm, d_ff // tf),
        in_specs=[pl.BlockSpec((tm, d), lambda i, f: (i, 0)),
                  pl.BlockSpec((tm, d), lambda i, f: (i, 0)),
                  pl.BlockSpec((d, tf), lambda i, f: (0, f)),
                  pl.BlockSpec((tf, d), lambda i, f: (f, 0)),
                  pl.BlockSpec((1, d), lambda i, f: (0, 0)),
                  whole(hs, pipeline_mode=pl.Buffered(1)),
                  whole(ys, pipeline_mode=pl.Buffered(1))],
        out_specs=[pl.BlockSpec((tm, d), lambda i, f: (i, 0)), whole(ys)],
        compiler_params=_params(("arbitrary", "arbitrary"), V7X_VMEM_BUDGET_FFN),
        name="ffn",
    )(h, y, w1, w2, g, hs, ys)


def _token_path_in(x, norm_g, w_in, ln_g, ln_b, d_attn, d_gmlp, tm, vg_dtype, v_kind):
    m, d_model = x.shape
    c = 0
    q, h = _inproj(x, w_in, c, d_attn, "q", [BF16, BF16], _row_tile(m, TM_QNORM), extra=(norm_g,),
                   scale=HEAD_DIM ** -0.5 * LOG2_E)
    c += d_attn
    k32, k16 = _inproj(h, w_in, c, d_attn, "kv", [F32, BF16], tm)
    c += d_attn
    v32, v16 = _inproj(h, w_in, c, d_attn, v_kind, [F32, BF16], tm)
    c += d_attn
    (u,) = _inproj(h, w_in, c, d_gmlp, "u", [BF16], tm)
    c += d_gmlp
    (vg,) = _inproj(h, w_in, c, d_gmlp, "vs", [vg_dtype], tm, extra=(ln_g, ln_b))
    c += d_gmlp
    (gates,) = _inproj(h, w_in, c, 2 * d_model, "gates", [BF16], tm)
    return q, k32, k16, v32, v16, u, vg, gates


def kernel(x_prompt, x_sample, cache_k, cache_v, page_table, norm_mix_g, w_in, lambda_q1, lambda_k1,
           lambda_q2, lambda_k2, subln_g, gmlp_ln_g, gmlp_ln_b, w_spatial, b_spatial, w_branch_a,
           w_branch_b, w_out, norm_ffn_g, w_ff1, w_ff2, norm_final_g):
    bp, sp, d_model = x_prompt.shape
    bs, ts, _ = x_sample.shape
    depth = w_in.shape[0]
    assert bp == 1 and depth == 1, "the final norm is fused into the MLP kernel of the single layer"
    n_heads = cache_k.shape[3]
    d_attn = n_heads * V_DIM
    d_gmlp = gmlp_ln_g.shape[1]
    n_groups = w_spatial.shape[1]
    page = cache_k.shape[2]
    assert sp % CHUNK == 0 and CHUNK % ts == 0 and (bs * ts) % CHUNK == 0

    l = 0
    lam_init = 0.8 - 0.6 * math.exp(-0.3 * l)
    row2 = lambda a: a.reshape(1, -1)
    lams = [row2(lambda_q1[l]), row2(lambda_k1[l]), row2(lambda_q2[l]), row2(lambda_k2[l])]
    sub_g = row2(subln_g[l])
    ln_g, ln_b = row2(gmlp_ln_g[l]), row2(gmlp_ln_b[l])
    mix_g, ffn_g, fin_g = row2(norm_mix_g[l]), row2(norm_ffn_g[l]), row2(norm_final_g)

    xp = x_prompt.reshape(sp, d_model)
    q, k32, k16, v32, v16t, u, vg, gates = _token_path_in(
        xp, mix_g, w_in[l], ln_g, ln_b, d_attn, d_gmlp, _row_tile(sp, TM_INPROJ), BF16, "vt")
    att, (wa, wb, wo, w1, w2) = _attn_prompt(
        q, k16, v16t, lams, subln_g[l].reshape(-1, 1), lam_init, _row_tile(sp, ATTN_TQ),
        [w_branch_a[l], w_branch_b[l], w_out[l], w_ff1[l], w_ff2[l]])
    bsp_p = b_spatial[l].reshape(n_groups, CHUNK, 1)
    yp_mid, hp2 = _mixer_out(att, u, vg, gates, xp, w_spatial[l], bsp_p, wa, wb, wo, ffn_g, CHUNK,
                             _row_tile(sp, TM_MIXER))
    new_k_prompt = k32.reshape(1, bp, sp, n_heads, V_DIM)
    new_v_prompt = v32.reshape(1, bp, sp, n_heads, V_DIM)

    ms = bs * ts
    xs = x_sample.reshape(ms, d_model)
    q, k32, k16, v32, v16, u, vg, gates = _token_path_in(
        xs, mix_g, w_in[l], ln_g, ln_b, d_attn, d_gmlp, _row_tile(ms, TM_INPROJ), F32, "kv")
    q5 = q.reshape(bs, ts, n_heads, 2, HEAD_DIM).transpose(0, 2, 3, 1, 4)
    eye = jnp.eye(2, dtype=BF16)
    q_rows = (q5[:, :, :, :, None, :] * eye[None, None, :, None, :, None]).reshape(
        bs, n_heads * 2 * ts, V_DIM)
    k_new = k16.reshape(bs, ts * n_heads, V_DIM)
    v_new = v16.reshape(bs, ts * n_heads, V_DIM)
    ck = cache_k[l].reshape(-1, page * n_heads, V_DIM)
    cv = cache_v[l].reshape(-1, page * n_heads, V_DIM)
    n_pages = page_table.shape[1]
    pages = math.gcd(n_pages, DECODE_PAGES)
    att_s = _attn_decode(q_rows, k_new, v_new, ck, cv, page_table, lams, sub_g, lam_init, ts,
                         n_heads, pages)
    att_s = att_s.reshape(ms, d_attn)
    reps = CHUNK // ts
    wsp_s = jnp.tile(w_spatial[l][:, :ts, :ts], (1, reps, reps))
    bsp_s = jnp.tile(b_spatial[l][:, :ts], (1, reps)).reshape(n_groups, CHUNK, 1)
    ys_mid, hs2 = _mixer_out(att_s, u, vg, gates, xs, wsp_s, bsp_s, wa, wb, wo, ffn_g, ts,
                             _row_tile(ms, TM_MIXER))
    yp, ys = _ffn(hp2, yp_mid, w1, w2, fin_g, _row_tile(sp, TM_FFN), hs2, ys_mid)

    return (yp.reshape(bp, sp, d_model), ys.reshape(bs, ts, d_model), new_k_prompt, new_v_prompt,
            k32.reshape(1, bs, ts, n_heads, V_DIM), v32.reshape(1, bs, ts, n_heads, V_DIM),
            vg.reshape(1, bs, ts, d_gmlp))
```
